```python
import math
import jax, jax.numpy as jnp
from jax import lax
import numpy as np

D_MODEL = 2048
BATCH = 4
SEQ = 4096
DEPTH = 2

CHUNK = 64
N_MIXERS = 2
N_SSD_LAYERS = (DEPTH + 1) // 2
N_FOX_LAYERS = DEPTH // 2
EPS = 1e-6
N_MOD = 6

SSD_EXPAND = 2
SSD_D_INNER = SSD_EXPAND * D_MODEL
SSD_HEAD_DIM = 64
SSD_HEADS = SSD_D_INNER // SSD_HEAD_DIM
SSD_GROUPS = 8
SSD_STATE = 128
SSD_CONV = 4
SSD_CONV_DIM = SSD_D_INNER + 2 * SSD_GROUPS * SSD_STATE
SSD_IN_DIM = SSD_D_INNER + SSD_CONV_DIM + SSD_HEADS

FOX_HEAD_DIM = 128
FOX_HEADS = D_MODEL // FOX_HEAD_DIM
FOX_WIDTH = FOX_HEADS * FOX_HEAD_DIM
FOX_IN_DIM = 3 * FOX_WIDTH + FOX_HEADS
Q_BLOCK = 128

D_FF = 4 * D_MODEL

kernel_name = 'hybrid_ssd_fox_adaln_trunk'


def rms_norm(x, g):
    xf = x.astype(jnp.float32)
    y = xf * lax.rsqrt(jnp.mean(xf * xf, axis=-1, keepdims=True) + EPS)
    return (y * g.astype(jnp.float32)).astype(x.dtype)


def modulate(h, shift, scale):
    return h * (1.0 + scale) + shift


def causal_depthwise_conv(x, w, bias):
    out = lax.conv_general_dilated(
        x, w[:, None, :].astype(x.dtype), window_strides=(1,),
        padding=[(SSD_CONV - 1, 0)], dimension_numbers=('NWC', 'WIO', 'NWC'),
        feature_group_count=x.shape[-1])
    return out + bias.astype(x.dtype)


def ssd_chunked_scan(xh, dt, A, Bg, Cg):
    b, l, h, p = xh.shape
    g, n = Bg.shape[2], Bg.shape[3]
    r = h // g
    nc = l // CHUNK
    x = xh.reshape(b, nc, CHUNK, g, r, p)
    dtc = dt.reshape(b, nc, CHUNK, g, r)
    xdt = x * dtc[..., None]
    a_cum = jnp.cumsum(dtc * A.reshape(g, r), axis=2)
    Bc = Bg.reshape(b, nc, CHUNK, g, n)
    Cc = Cg.reshape(b, nc, CHUNK, g, n)
    causal = jnp.tril(jnp.ones((CHUNK, CHUNK), dtype=bool))
    seg = a_cum[:, :, :, None] - a_cum[:, :, None, :]
    decay = jnp.exp(jnp.where(causal[None, None, :, :, None, None], seg, -jnp.inf))
    scores = jnp.einsum('bctgn,bcsgn->bctsg', Cc, Bc)
    y_diag = jnp.einsum('bctsgr,bcsgrp->bctgrp', scores[..., None] * decay, xdt)
    decay_to_end = jnp.exp(a_cum[:, :, -1:] - a_cum)
    states = jnp.einsum('bcsgn,bcsgrp->bcgrpn', Bc, xdt * decay_to_end[..., None])
    chunk_decay = jnp.exp(a_cum[:, :, -1])

    def step(carry, inp):
        st, dec = inp
        return carry * dec[..., None, None] + st, carry

    init = jnp.zeros((b, g, r, p, n), jnp.float32)
    _, prev = lax.scan(step, init, (jnp.moveaxis(states, 1, 0), jnp.moveaxis(chunk_decay, 1, 0)))
    prev = jnp.moveaxis(prev, 0, 1)
    y_off = jnp.einsum('bctgn,bcgrpn->bctgrp', Cc, prev) * jnp.exp(a_cum)[..., None]
    return (y_diag + y_off).reshape(b, l, h, p)


def ssd_mixer(u, w_in, conv_w, conv_b, dt_bias, A_log, D_skip, gnorm, w_out):
    b, l, _ = u.shape
    zxbcdt = u @ w_in
    z, xbc, dt_raw = jnp.split(zxbcdt, [SSD_D_INNER, SSD_D_INNER + SSD_CONV_DIM], axis=-1)
    xbc = jax.nn.silu(causal_depthwise_conv(xbc, conv_w, conv_b))
    xs, Bm, Cm = jnp.split(xbc, [SSD_D_INNER, SSD_D_INNER + SSD_GROUPS * SSD_STATE], axis=-1)
    dt = jax.nn.softplus(dt_raw.astype(jnp.float32) + dt_bias.astype(jnp.float32))
    A = -jnp.exp(A_log.astype(jnp.float32))
    xh = xs.reshape(b, l, SSD_HEADS, SSD_HEAD_DIM).astype(jnp.float32)
    y = ssd_chunked_scan(
        xh, dt, A,
        Bm.reshape(b, l, SSD_GROUPS, SSD_STATE).astype(jnp.float32),
        Cm.reshape(b, l, SSD_GROUPS, SSD_STATE).astype(jnp.float32))
    y = y + xh * D_skip.astype(jnp.float32)[:, None]
    y = y.reshape(b, l, SSD_D_INNER) * jax.nn.silu(z.astype(jnp.float32))
    yg = y.reshape(b, l, SSD_GROUPS, SSD_D_INNER // SSD_GROUPS)
    yg = yg * lax.rsqrt(jnp.mean(yg * yg, axis=-1, keepdims=True) + EPS)
    y = yg.reshape(b, l, SSD_D_INNER) * gnorm.astype(jnp.float32)
    return y.astype(u.dtype) @ w_out


def fox_mixer(u, w_in, b_f, w_out):
    b, l, _ = u.shape
    qkvf = u @ w_in
    q, k, v, f_logit = jnp.split(qkvf, [FOX_WIDTH, 2 * FOX_WIDTH, 3 * FOX_WIDTH], axis=-1)
    q = q.reshape(b, l, FOX_HEADS, FOX_HEAD_DIM)
    k = k.reshape(b, l, FOX_HEADS, FOX_HEAD_DIM)
    v = v.reshape(b, l, FOX_HEADS, FOX_HEAD_DIM)
    log_f = jax.nn.log_sigmoid(f_logit.astype(jnp.float32) + b_f.astype(jnp.float32))
    cum = jnp.moveaxis(jnp.cumsum(log_f, axis=1), 1, 2)
    scale = FOX_HEAD_DIM ** -0.5
    outs = []
    for blk in range(l // Q_BLOCK):
        q0 = blk * Q_BLOCK
        q1 = q0 + Q_BLOCK
        s = jnp.einsum('bqhd,bkhd->bhqk', q[:, q0:q1], k[:, :q1]).astype(jnp.float32) * scale
        s = s + cum[:, :, q0:q1, None] - cum[:, :, None, :q1]
        causal = jnp.arange(q0, q1)[:, None] >= jnp.arange(q1)[None, :]
        pr = jax.nn.softmax(jnp.where(causal, s, -jnp.inf), axis=-1)
        outs.append(jnp.einsum('bhqk,bkhd->bqhd', pr.astype(v.dtype), v[:, :q1]))
    o = jnp.concatenate(outs, axis=1).reshape(b, l, FOX_WIDTH)
    return o @ w_out


def sq_relu_mlp(u, w_up, w_down):
    return jnp.square(jax.nn.relu(u @ w_up)) @ w_down


def setup_inputs(seed: int = 0) -> dict:
    key = jax.random.key(seed)
    ks = jax.random.split(key, 24)
    f32 = jnp.float32

    def nrm(k, shape, fan_in, s=1.0):
        return jax.random.normal(k, shape, f32) * (s * fan_in ** -0.5)

    def gain(k, shape):
        return 1.0 + 0.05 * jax.random.normal(k, shape, f32)

    x = jax.random.normal(ks[0], (BATCH, SEQ, D_MODEL), f32)
    c = jax.random.normal(ks[1], (BATCH, D_MODEL), f32)
    norm_mix = gain(ks[2], (DEPTH, D_MODEL))
    norm_mlp = gain(ks[3], (DEPTH, D_MODEL))
    w_ada = nrm(ks[4], (DEPTH, D_MODEL, N_MOD * D_MODEL), D_MODEL, 0.5)
    b_ada = 0.02 * jax.random.normal(ks[5], (DEPTH, N_MOD * D_MODEL), f32)
    w_up = nrm(ks[6], (DEPTH, D_MODEL, D_FF), D_MODEL)
    w_down = nrm(ks[7], (DEPTH, D_FF, D_MODEL), D_FF)
    ssd_w_in = nrm(ks[8], (N_SSD_LAYERS, D_MODEL, SSD_IN_DIM), D_MODEL)
    ssd_conv_w = nrm(ks[9], (N_SSD_LAYERS, SSD_CONV, SSD_CONV_DIM), SSD_CONV)
    ssd_conv_b = 0.02 * jax.random.normal(ks[10], (N_SSD_LAYERS, SSD_CONV_DIM), f32)
    dt0 = jnp.exp(jax.random.uniform(ks[11], (N_SSD_LAYERS, SSD_HEADS), f32,
                                     math.log(1e-3), math.log(1e-1)))
    ssd_dt_bias = dt0 + jnp.log(-jnp.expm1(-dt0))
    ssd_A_log = jnp.log(jax.random.uniform(ks[12], (N_SSD_LAYERS, SSD_HEADS), f32, 1.0, 16.0))
    ssd_D = gain(ks[13], (N_SSD_LAYERS, SSD_HEADS))
    ssd_gnorm = gain(ks[14], (N_SSD_LAYERS, SSD_D_INNER))
    ssd_w_out = nrm(ks[15], (N_SSD_LAYERS, SSD_D_INNER, D_MODEL), SSD_D_INNER)
    fox_w_in = nrm(ks[16], (N_FOX_LAYERS, D_MODEL, FOX_IN_DIM), D_MODEL)
    fox_b_f = jax.random.uniform(ks[17], (N_FOX_LAYERS, FOX_HEADS), f32, 1.0, 4.0)
    fox_w_out = nrm(ks[18], (N_FOX_LAYERS, FOX_WIDTH, D_MODEL), FOX_WIDTH)
    final_norm = gain(ks[19], (D_MODEL,))
    return {'x': x, 'c': c, 'norm_mix': norm_mix, 'norm_mlp': norm_mlp,
            'w_ada': w_ada, 'b_ada': b_ada, 'w_up': w_up, 'w_down': w_down,
            'ssd_w_in': ssd_w_in, 'ssd_conv_w': ssd_conv_w, 'ssd_conv_b': ssd_conv_b,
            'ssd_dt_bias': ssd_dt_bias, 'ssd_A_log': ssd_A_log, 'ssd_D': ssd_D,
            'ssd_gnorm': ssd_gnorm, 'ssd_w_out': ssd_w_out,
            'fox_w_in': fox_w_in, 'fox_b_f': fox_b_f, 'fox_w_out': fox_w_out,
            'final_norm': final_norm}


def reference(x, c, norm_mix, norm_mlp, w_ada, b_ada, w_up, w_down,
              ssd_w_in, ssd_conv_w, ssd_conv_b, ssd_dt_bias, ssd_A_log, ssd_D,
              ssd_gnorm, ssd_w_out, fox_w_in, fox_b_f, fox_w_out, final_norm):
    cond = jax.nn.silu(c)
    h = x
    for i in range(DEPTH):
        mod = cond @ w_ada[i] + b_ada[i]
        sh_a, sc_a, g_a, sh_f, sc_f, g_f = jnp.split(mod[:, None, :], N_MOD, axis=-1)
        u = modulate(rms_norm(h, norm_mix[i]), sh_a, sc_a)
        j = i // N_MIXERS
        if i % N_MIXERS == 0:
            mix = ssd_mixer(u, ssd_w_in[j], ssd_conv_w[j], ssd_conv_b[j], ssd_dt_bias[j],
                            ssd_A_log[j], ssd_D[j], ssd_gnorm[j], ssd_w_out[j])
        else:
            mix = fox_mixer(u, fox_w_in[j], fox_b_f[j], fox_w_out[j])
        h = h + g_a * mix
        u = modulate(rms_norm(h, norm_mlp[i]), sh_f, sc_f)
        h = h + g_f * sq_relu_mlp(u, w_up[i], w_down[i])
    return rms_norm(h, final_norm)
```

```python
import functools

import jax
import jax.numpy as jnp
from jax import lax
from jax.experimental import pallas as pl
from jax.experimental.pallas import tpu as pltpu

F32 = jnp.float32
BF16 = jnp.bfloat16

EPS = 1e-6
N_MOD = 6
LANES = 128
SSD_HEAD_DIM = 64
SSD_STATE = 128
SSD_CONV = 4
SSD_HEADS_PER_GROUP = 8
SSD_GROUP_WIDTH = SSD_HEADS_PER_GROUP * SSD_HEAD_DIM
SSD_CHUNK = 128
FOX_HEAD_DIM = 128
VMEM_LIMIT = 56 * 1024 * 1024


def _cparams(sem):
    return pltpu.CompilerParams(dimension_semantics=sem, vmem_limit_bytes=VMEM_LIMIT)


def _silu(x):
    return x * (1.0 / (1.0 + jnp.exp(-x)))


def _softplus(x):
    return jnp.maximum(x, 0.0) + jnp.log1p(jnp.exp(-jnp.abs(x)))


def _split3(x):
    hi = x.astype(BF16)
    r1 = x - hi.astype(F32)
    mid = r1.astype(BF16)
    lo = (r1 - mid.astype(F32)).astype(BF16)
    return hi, mid, lo


def _cumsum_rows(x):
    n = x.shape[0]
    row = lax.broadcasted_iota(jnp.int32, x.shape, 0)
    k = 1
    while k < n:
        x = x + jnp.where(row >= k, pltpu.roll(x, k, 0), 0.0)
        k *= 2
    return x


def _norm_mod(x, g, shift, scale):
    ms = jnp.mean(x * x, axis=-1, keepdims=True)
    y = x * lax.rsqrt(ms + EPS) * g
    return y * (1.0 + scale) + shift


def _ada_kernel(c_ref, w_ref, b_ref, o_ref):
    cond = _silu(c_ref[...])
    o_ref[0] = jnp.dot(cond.astype(BF16), w_ref[0].astype(BF16),
                       preferred_element_type=F32) + b_ref[0]


def _ada(c_pad, w_ada, b_ada, tn):
    depth, d, n = w_ada.shape
    rows = c_pad.shape[0]
    return pl.pallas_call(
        _ada_kernel,
        grid=(depth, n // tn),
        in_specs=[
            pl.BlockSpec((rows, d), lambda l, j: (0, 0)),
            pl.BlockSpec((1, d, tn), lambda l, j: (l, 0, j)),
            pl.BlockSpec((1, 1, tn), lambda l, j: (l, 0, j)),
        ],
        out_specs=pl.BlockSpec((1, rows, tn), lambda l, j: (l, 0, j)),
        out_shape=jax.ShapeDtypeStruct((depth, rows, n), F32),
        compiler_params=_cparams(("parallel", "parallel")),
        name="ada",
    )(c_pad, w_ada, b_ada.reshape(depth, 1, n))


def _in_proj_kernel(x_ref, g_ref, sh_ref, sc_ref, w_ref, wx_ref, o_ref, ox_ref, u_ref,
                    *, n_scaled, scale):
    j = pl.program_id(1)

    @pl.when(j == 0)
    def _():
        u = _norm_mod(x_ref[...], g_ref[...], sh_ref[0], sc_ref[0]).astype(BF16)
        u_ref[...] = u
        ox_ref[...] = jnp.dot(u, wx_ref[...], preferred_element_type=F32)

    acc = jnp.dot(u_ref[...], w_ref[...], preferred_element_type=F32)
    if n_scaled:
        acc = acc * jnp.where(j < n_scaled, scale, 1.0)
    o_ref[0] = acc.astype(o_ref.dtype)


def _in_proj(h, g, mod, sh_row, sc_row, w, wx, seq, tm, tn, n_scaled=0, scale=1.0):
    m, d = h.shape
    n = w.shape[1]
    per_b = seq // tm
    return pl.pallas_call(
        functools.partial(_in_proj_kernel, n_scaled=n_scaled, scale=scale),
        grid=(m // tm, n // tn),
        in_specs=[
            pl.BlockSpec((tm, d), lambda i, j: (i, 0)),
            pl.BlockSpec((1, d), lambda i, j: (0, 0)),
            pl.BlockSpec((1, 1, d), lambda i, j: (sh_row + i // per_b, 0, 0)),
            pl.BlockSpec((1, 1, d), lambda i, j: (sc_row + i // per_b, 0, 0)),
            pl.BlockSpec((d, tn), lambda i, j: (0, j)),
            pl.BlockSpec((d, LANES), lambda i, j: (0, 0)),
        ],
        out_specs=[
            pl.BlockSpec((1, tm, tn), lambda i, j: (j, i, 0)),
            pl.BlockSpec((tm, LANES), lambda i, j: (i, 0)),
        ],
        out_shape=[
            jax.ShapeDtypeStruct((n // tn, m, tn), BF16),
            jax.ShapeDtypeStruct((m, LANES), F32),
        ],
        scratch_shapes=[pltpu.VMEM((tm, d), BF16)],
        compiler_params=_cparams(("parallel", "arbitrary")),
        name="in_proj",
    )(h, g, mod, mod, w, wx)


def _out_proj_kernel(a_ref, w_ref, h_ref, gate_ref, o_ref):
    acc = jnp.dot(a_ref[...], w_ref[...], preferred_element_type=F32)
    o_ref[...] = h_ref[...] + gate_ref[0] * acc


def _out_proj(a, w, h, mod, gate_row, seq, tm, tn):
    m, k = a.shape
    d = w.shape[1]
    per_b = seq // tm
    return pl.pallas_call(
        _out_proj_kernel,
        grid=(m // tm, d // tn),
        in_specs=[
            pl.BlockSpec((tm, k), lambda i, j: (i, 0)),
            pl.BlockSpec((k, tn), lambda i, j: (0, j)),
            pl.BlockSpec((tm, tn), lambda i, j: (i, j)),
            pl.BlockSpec((1, 1, tn), lambda i, j: (gate_row + i // per_b, 0, j)),
        ],
        out_specs=pl.BlockSpec((tm, tn), lambda i, j: (i, j)),
        out_shape=jax.ShapeDtypeStruct((m, d), F32),
        compiler_params=_cparams(("parallel", "arbitrary")),
        name="out_proj",
    )(a, w, h, mod)


def _mlp_kernel(h_ref, g_ref, sh_ref, sc_ref, gate_ref, wu_ref, wd_ref, fn_ref, o_ref, u_ref,
                *, final_norm):
    f = pl.program_id(1)

    @pl.when(f == 0)
    def _():
        u_ref[...] = _norm_mod(h_ref[...], g_ref[...], sh_ref[0], sc_ref[0]).astype(BF16)
        o_ref[...] = jnp.zeros_like(o_ref)

    a = jnp.dot(u_ref[...], wu_ref[...], preferred_element_type=F32)
    a = jnp.square(jnp.maximum(a, 0.0)).astype(BF16)
    o_ref[...] += jnp.dot(a, wd_ref[...], preferred_element_type=F32)

    @pl.when(f == pl.num_programs(1) - 1)
    def _():
        y = h_ref[...] + gate_ref[0] * o_ref[...]
        if final_norm:
            ms = jnp.mean(y * y, axis=-1, keepdims=True)
            y = y * lax.rsqrt(ms + EPS) * fn_ref[...]
        o_ref[...] = y


def _mlp(h, g, mod, sh_row, sc_row, gate_row, w_up, w_down, fn, seq, tm, tf, final_norm):
    m, d = h.shape
    dff = w_up.shape[1]
    per_b = seq // tm
    return pl.pallas_call(
        functools.partial(_mlp_kernel, final_norm=final_norm),
        grid=(m // tm, dff // tf),
        in_specs=[
            pl.BlockSpec((tm, d), lambda i, f: (i, 0)),
            pl.BlockSpec((1, d), lambda i, f: (0, 0)),
            pl.BlockSpec((1, 1, d), lambda i, f: (sh_row + i // per_b, 0, 0)),
            pl.BlockSpec((1, 1, d), lambda i, f: (sc_row + i // per_b, 0, 0)),
            pl.BlockSpec((1, 1, d), lambda i, f: (gate_row + i // per_b, 0, 0)),
            pl.BlockSpec((d, tf), lambda i, f: (0, f)),
            pl.BlockSpec((tf, d), lambda i, f: (f, 0)),
            pl.BlockSpec((1, d), lambda i, f: (0, 0)),
        ],
        out_specs=pl.BlockSpec((tm, d), lambda i, f: (i, 0)),
        out_shape=jax.ShapeDtypeStruct((m, d), F32),
        scratch_shapes=[pltpu.VMEM((tm, d), BF16)],
        compiler_params=_cparams(("parallel", "arbitrary")),
        name="mlp",
    )(h, g, mod, mod, mod, w_up, w_down, fn)


def _ssd_kernel(z_ref, x_ref, b_ref, c_ref, dt_ref, wx_ref, wb_ref, wc_ref, bx_ref, bb_ref, bc_ref,
                dtb_ref, alog_ref, d_ref, gn_ref, o_ref, state_ref, xp_ref, bp_ref, cp_ref, *, t_blk):
    g = pl.program_id(1)
    t = pl.program_id(2)
    ck = SSD_CHUNK
    gw = SSD_GROUP_WIDTH

    @pl.when(t == 0)
    def _():
        state_ref[...] = jnp.zeros_like(state_ref)
        xp_ref[0:8, :] = jnp.zeros((8, gw), F32)
        bp_ref[0:8, :] = jnp.zeros((8, SSD_STATE), F32)
        cp_ref[0:8, :] = jnp.zeros((8, SSD_STATE), F32)

    xp_ref[8:8 + t_blk, :] = x_ref[0].astype(F32)
    bp_ref[8:8 + t_blk, :] = b_ref[0].astype(F32)
    cp_ref[8:8 + t_blk, :] = c_ref[0].astype(F32)

    def conv(pad_ref, w_ref, bias_ref, r0):
        acc = bias_ref[...] + w_ref[0:1, :] * pad_ref[pl.ds(r0 + 5, ck), :]
        for k in range(1, SSD_CONV):
            acc = acc + w_ref[k:k + 1, :] * pad_ref[pl.ds(r0 + 5 + k, ck), :]
        return _silu(acc)

    row = lax.broadcasted_iota(jnp.int32, (ck, ck), 0)
    col = lax.broadcasted_iota(jnp.int32, (ck, ck), 1)
    causal = row >= col
    low_half = col < SSD_HEAD_DIM
    e_h = lax.broadcasted_iota(jnp.int32, (LANES, gw), 0)
    e_c = lax.broadcasted_iota(jnp.int32, (LANES, gw), 1)
    expand = (e_h == g * SSD_HEADS_PER_GROUP + (e_c >> 6)).astype(BF16)
    s_j = lax.broadcasted_iota(jnp.int32, (16, LANES), 0)
    s_h = lax.broadcasted_iota(jnp.int32, (16, LANES), 1)
    select = ((s_h == g * SSD_HEADS_PER_GROUP + s_j) & (s_j < SSD_HEADS_PER_GROUP)).astype(BF16)

    a_neg = -jnp.exp(alog_ref[...])
    nt = (((1,), (1,)), ((), ()))
    tn = (((0,), (0,)), ((), ()))

    for c in range(t_blk // ck):
        r0 = c * ck
        xs = conv(xp_ref, wx_ref, bx_ref, r0)
        bm = conv(bp_ref, wb_ref, bb_ref, r0).astype(BF16)
        cm = conv(cp_ref, wc_ref, bc_ref, r0).astype(BF16)

        dt = _softplus(dt_ref[r0:r0 + ck, :] + dtb_ref[...])
        acum = _cumsum_rows(dt * a_neg)
        a_last = acum[ck - 1:ck, :]
        stack = jnp.concatenate(
            [dt, dt * jnp.exp(a_last - acum), jnp.exp(acum), acum,
             jnp.broadcast_to(jnp.exp(a_last), (8, LANES))], axis=0)
        s_hi, s_mid, s_lo = _split3(stack)
        ex = (jnp.dot(s_hi, expand, preferred_element_type=F32)
              + jnp.dot(s_mid, expand, preferred_element_type=F32)
              + jnp.dot(s_lo, expand, preferred_element_type=F32))
        dt_x = ex[0:ck]
        wend_x = ex[ck:2 * ck]
        ea_x = ex[2 * ck:3 * ck]
        acum_x = ex[3 * ck:4 * ck]
        cd_x = ex[4 * ck:4 * ck + 1]
        a_hi, a_mid, a_lo = _split3(acum)
        acum_t = (lax.dot_general(select, a_hi, nt, preferred_element_type=F32)
                  + lax.dot_general(select, a_mid, nt, preferred_element_type=F32)
                  + lax.dot_general(select, a_lo, nt, preferred_element_type=F32))

        scores = lax.dot_general(cm, bm, nt, preferred_element_type=F32)
        xdt = xs * dt_x
        y_pairs = []
        for p in range(SSD_HEADS_PER_GROUP // 2):
            x_pair = xdt[:, p * LANES:(p + 1) * LANES]
            halves = (jnp.where(low_half, x_pair, 0.0).astype(BF16),
                      jnp.where(low_half, 0.0, x_pair).astype(BF16))
            y_p = None
            for q in range(2):
                j = 2 * p + q
                a_col = jnp.broadcast_to(acum_x[:, j * SSD_HEAD_DIM:j * SSD_HEAD_DIM + 1], (ck, ck))
                seg = a_col - acum_t[j:j + 1, :]
                decay = jnp.exp(jnp.where(causal, seg, -jnp.inf))
                mh = (scores * decay).astype(BF16)
                y_q = jnp.dot(mh, halves[q], preferred_element_type=F32)
                y_p = y_q if y_p is None else y_p + y_q
            y_pairs.append(y_p)
        y = jnp.concatenate(y_pairs, axis=1)

        state = state_ref[...]
        y = y + jnp.dot(cm, state.astype(BF16), preferred_element_type=F32) * ea_x
        state_ref[...] = state * cd_x + lax.dot_general(
            bm, (xs * wend_x).astype(BF16), tn, preferred_element_type=F32)

        y = y + xs * d_ref[...]
        y = y * _silu(z_ref[0, r0:r0 + ck, :].astype(F32))
        ms = jnp.mean(y * y, axis=-1, keepdims=True)
        y = y * lax.rsqrt(ms + EPS) * gn_ref[...]
        o_ref[r0:r0 + ck, :] = y.astype(o_ref.dtype)

    xp_ref[0:8, :] = xp_ref[t_blk:t_blk + 8, :]
    bp_ref[0:8, :] = bp_ref[t_blk:t_blk + 8, :]
    cp_ref[0:8, :] = cp_ref[t_blk:t_blk + 8, :]


def _ssd(zxbc, dt_raw, conv_w, conv_b, dtb, alog, d_x, gnorm, batch, seq, t_blk, tn_in):
    m = dt_raw.shape[0]
    gw = SSD_GROUP_WIDTH
    d_inner = gnorm.shape[1]
    groups = d_inner // gw
    n_t = seq // t_blk
    per_tile = tn_in // gw
    x_tile0 = d_inner // tn_in
    b_tile = 2 * d_inner // tn_in
    c_tile = b_tile + 1
    assert groups * SSD_STATE == tn_in and tn_in % gw == 0
    xw0 = 0
    bw0 = d_inner // SSD_STATE
    cw0 = bw0 + groups

    def rows(b, t):
        return b * n_t + t

    return pl.pallas_call(
        functools.partial(_ssd_kernel, t_blk=t_blk),
        grid=(batch, groups, n_t),
        in_specs=[
            pl.BlockSpec((1, t_blk, gw), lambda b, g, t: (g // per_tile, rows(b, t), g % per_tile)),
            pl.BlockSpec((1, t_blk, gw), lambda b, g, t: (x_tile0 + g // per_tile, rows(b, t), g % per_tile)),
            pl.BlockSpec((1, t_blk, SSD_STATE), lambda b, g, t: (b_tile, rows(b, t), g)),
            pl.BlockSpec((1, t_blk, SSD_STATE), lambda b, g, t: (c_tile, rows(b, t), g)),
            pl.BlockSpec((t_blk, LANES), lambda b, g, t: (rows(b, t), 0)),
            pl.BlockSpec((SSD_CONV, gw), lambda b, g, t: (0, xw0 + g)),
            pl.BlockSpec((SSD_CONV, SSD_STATE), lambda b, g, t: (0, bw0 + g)),
            pl.BlockSpec((SSD_CONV, SSD_STATE), lambda b, g, t: (0, cw0 + g)),
            pl.BlockSpec((1, gw), lambda b, g, t: (0, xw0 + g)),
            pl.BlockSpec((1, SSD_STATE), lambda b, g, t: (0, bw0 + g)),
            pl.BlockSpec((1, SSD_STATE), lambda b, g, t: (0, cw0 + g)),
            pl.BlockSpec((1, LANES), lambda b, g, t: (0, 0)),
            pl.BlockSpec((1, LANES), lambda b, g, t: (0, 0)),
            pl.BlockSpec((1, gw), lambda b, g, t: (0, g)),
            pl.BlockSpec((1, gw), lambda b, g, t: (0, g)),
        ],
        out_specs=pl.BlockSpec((t_blk, gw), lambda b, g, t: (rows(b, t), g)),
        out_shape=jax.ShapeDtypeStruct((m, d_inner), BF16),
        scratch_shapes=[
            pltpu.VMEM((SSD_STATE, gw), F32),
            pltpu.VMEM((t_blk + 8, gw), F32),
            pltpu.VMEM((t_blk + 8, SSD_STATE), F32),
            pltpu.VMEM((t_blk + 8, SSD_STATE), F32),
        ],
        compiler_params=_cparams(("parallel", "parallel", "arbitrary")),
        name="ssd",
    )(zxbc, zxbc, zxbc, zxbc, dt_raw, conv_w, conv_w, conv_w, conv_b, conv_b, conv_b,
      dtb, alog, d_x, gnorm)


def _fcum_kernel(f_ref, b_ref, o_ref):
    x = f_ref[...] + b_ref[...]
    o_ref[0] = _cumsum_rows(-_softplus(-x))


def _fcum(f_logit, b_f, batch, seq):
    return pl.pallas_call(
        _fcum_kernel,
        grid=(batch,),
        in_specs=[
            pl.BlockSpec((seq, LANES), lambda b: (b, 0)),
            pl.BlockSpec((1, LANES), lambda b: (0, 0)),
        ],
        out_specs=pl.BlockSpec((1, seq, LANES), lambda b: (b, 0, 0)),
        out_shape=jax.ShapeDtypeStruct((batch, seq, LANES), F32),
        compiler_params=_cparams(("parallel",)),
        name="fcum",
    )(f_logit, b_f)


def _bias_columns(cum, head, first_col, sign):
    r = lax.broadcasted_iota(jnp.int32, (LANES, LANES), 0)
    c = lax.broadcasted_iota(jnp.int32, (LANES, LANES), 1)
    pieces = _split3(cum)
    out = None
    for i, piece in enumerate(pieces):
        sel = ((r == head) & (c == first_col + i)).astype(BF16)
        term = jnp.dot(piece, sel, preferred_element_type=F32)
        out = term if out is None else out + term
    lane = lax.broadcasted_iota(jnp.int32, out.shape, 1)
    ones_col = 3 - first_col
    ones = ((lane >= ones_col) & (lane < ones_col + 3)).astype(F32)
    return (sign * out + ones).astype(BF16)


def _attn_kernel(q_ref, k_ref, v_ref, cq_ref, ck_ref, o_ref, kx_ref, m_ref, l_ref, acc_ref,
                 *, tq, seq):
    head = pl.program_id(1)
    qi = pl.program_id(2)
    hd = FOX_HEAD_DIM

    @pl.when(qi == 0)
    def _():
        kx_ref[:, 0:hd] = k_ref[0]

        def fill(rc, carry):
            r0 = pl.multiple_of(rc * tq, tq)
            kx_ref[pl.ds(r0, tq), hd:2 * hd] = _bias_columns(ck_ref[0, pl.ds(r0, tq), :], head, 3, -1.0)
            return carry

        lax.fori_loop(0, seq // tq, fill, 0)

    qx = jnp.concatenate([q_ref[0], _bias_columns(cq_ref[0], head, 0, 1.0)], axis=1)
    nt = (((1,), (1,)), ((), ()))

    m_ref[...] = jnp.full(m_ref.shape, -jnp.inf, F32)
    l_ref[...] = jnp.zeros_like(l_ref)
    acc_ref[...] = jnp.zeros_like(acc_ref)

    def step(kb, masked):
        r0 = pl.multiple_of(kb * tq, tq)
        s = lax.dot_general(qx, kx_ref[pl.ds(r0, tq), :], nt, preferred_element_type=F32)
        if masked:
            row = lax.broadcasted_iota(jnp.int32, (tq, tq), 0)
            col = lax.broadcasted_iota(jnp.int32, (tq, tq), 1)
            s = jnp.where(row >= col, s, -jnp.inf)
        m_old = m_ref[...]
        m_new = jnp.maximum(m_old, jnp.max(s, axis=-1, keepdims=True))
        alpha = jnp.exp(m_old - m_new)
        p = jnp.exp(s - m_new)
        l_ref[...] = alpha * l_ref[...] + jnp.sum(p, axis=-1, keepdims=True)
        acc_ref[...] = alpha * acc_ref[...] + jnp.dot(
            p.astype(BF16), v_ref[0, pl.ds(r0, tq), :], preferred_element_type=F32)
        m_ref[...] = m_new

    def body(kb, carry):
        step(kb, False)
        return carry

    lax.fori_loop(0, qi, body, 0)
    step(qi, True)
    o_ref[...] = (acc_ref[...] / l_ref[...]).astype(o_ref.dtype)


def _attn(qkv, cum, batch, seq, heads, tq, tn_in):
    m = qkv.shape[1]
    hd = FOX_HEAD_DIM
    per_tile = tn_in // hd
    tiles = heads // per_tile
    n_q = seq // tq
    return pl.pallas_call(
        functools.partial(_attn_kernel, tq=tq, seq=seq),
        grid=(batch, heads, n_q),
        in_specs=[
            pl.BlockSpec((1, tq, hd), lambda b, h, i: (h // per_tile, b * n_q + i, h % per_tile)),
            pl.BlockSpec((1, seq, hd), lambda b, h, i: (tiles + h // per_tile, b, h % per_tile)),
            pl.BlockSpec((1, seq, hd), lambda b, h, i: (2 * tiles + h // per_tile, b, h % per_tile)),
            pl.BlockSpec((1, tq, LANES), lambda b, h, i: (b, i, 0)),
            pl.BlockSpec((1, seq, LANES), lambda b, h, i: (b, 0, 0)),
        ],
        out_specs=pl.BlockSpec((tq, hd), lambda b, h, i: (b * n_q + i, h)),
        out_shape=jax.ShapeDtypeStruct((m, heads * hd), BF16),
        scratch_shapes=[
            pltpu.VMEM((seq, 2 * hd), BF16),
            pltpu.VMEM((tq, 1), F32),
            pltpu.VMEM((tq, 1), F32),
            pltpu.VMEM((tq, hd), F32),
        ],
        compiler_params=_cparams(("parallel", "parallel", "arbitrary")),
        name="attn",
    )(qkv, qkv, qkv, cum, cum)


def _pad_cols(a, n):
    return jnp.pad(a, ((0, 0), (0, n - a.shape[1])))


def kernel(x, c, norm_mix, norm_mlp, w_ada, b_ada, w_up, w_down, ssd_w_in, ssd_conv_w, ssd_conv_b,
           ssd_dt_bias, ssd_A_log, ssd_D, ssd_gnorm, ssd_w_out, fox_w_in, fox_b_f, fox_w_out,
           final_norm):
    batch, seq, d = x.shape
    m = batch * seq
    depth = w_ada.shape[0]
    d_inner = ssd_gnorm.shape[1]
    ssd_heads = ssd_dt_bias.shape[1]
    fox_heads = fox_b_f.shape[1]
    fox_width = fox_heads * FOX_HEAD_DIM
    assert d_inner == ssd_heads * SSD_HEAD_DIM and ssd_heads <= LANES and fox_heads <= LANES

    tm = min(1024, seq)
    tn_in = (d_inner // SSD_GROUP_WIDTH) * SSD_STATE
    tn_out = min(512, d)
    tf = min(512, w_up.shape[2])
    t_blk = min(256, seq)
    tq = min(512, seq)
    assert fox_width % tn_in == 0 and d_inner % tn_in == 0

    rows = 8
    c_pad = jnp.pad(c, ((0, rows - batch), (0, 0)))
    mod = _ada(c_pad, w_ada, b_ada, tn=min(1024, N_MOD * d))[:, :batch]
    mod = mod.reshape(depth, batch, N_MOD, d).transpose(0, 2, 1, 3).reshape(depth * N_MOD * batch, 1, d)

    def mod_row(layer, which):
        return (layer * N_MOD + which) * batch

    h = x.reshape(m, d)
    for layer in range(depth):
        j = layer // 2
        g_mix = norm_mix[layer].reshape(1, d)
        g_mlp = norm_mlp[layer].reshape(1, d)
        if layer % 2 == 0:
            n_main = 2 * d_inner + 2 * (d_inner // SSD_GROUP_WIDTH) * SSD_STATE
            w_main = ssd_w_in[j][:, :n_main].astype(BF16)
            w_side = _pad_cols(ssd_w_in[j][:, n_main:], LANES).astype(BF16)
            zxbc, dt_raw = _in_proj(h, g_mix, mod, mod_row(layer, 0), mod_row(layer, 1),
                                    w_main, w_side, seq, tm, tn_in)
            y = _ssd(zxbc, dt_raw, ssd_conv_w[j], ssd_conv_b[j].reshape(1, -1),
                     _pad_cols(ssd_dt_bias[j].reshape(1, -1), LANES),
                     _pad_cols(ssd_A_log[j].reshape(1, -1), LANES),
                     jnp.repeat(ssd_D[j], SSD_HEAD_DIM).reshape(1, d_inner),
                     ssd_gnorm[j].reshape(1, d_inner), batch, seq, t_blk, tn_in)
            h = _out_proj(y, ssd_w_out[j].astype(BF16), h, mod, mod_row(layer, 2), seq, tm, tn_out)
        else:
            w_main = fox_w_in[j][:, :3 * fox_width].astype(BF16)
            w_side = _pad_cols(fox_w_in[j][:, 3 * fox_width:], LANES).astype(BF16)
            qkv, f_logit = _in_proj(h, g_mix, mod, mod_row(layer, 0), mod_row(layer, 1),
                                    w_main, w_side, seq, tm, tn_in,
                                    n_scaled=fox_width // tn_in, scale=FOX_HEAD_DIM ** -0.5)
            cum = _fcum(f_logit, _pad_cols(fox_b_f[j].reshape(1, -1), LANES), batch, seq)
            o = _attn(qkv, cum, batch, seq, fox_heads, tq, tn_in)
            h = _out_proj(o, fox_w_out[j].astype(BF16), h, mod, mod_row(layer, 2), seq, tm, tn_out)
        last = layer == depth - 1
        h = _mlp(h, g_mlp, mod, mod_row(layer, 3), mod_row(layer, 4), mod_row(layer, 5),
                 w_up[layer].astype(BF16), w_down[layer].astype(BF16), final_norm.reshape(1, d),
                 seq, tm, tf, final_norm=last)
    return h.reshape(batch, seq, d)
```

```python
import functools

import jax
import jax.numpy as jnp
from jax import lax
from jax.experimental import pallas as pl
from jax.experimental.pallas import tpu as pltpu

F32 = jnp.float32
BF16 = jnp.bfloat16

EPS = 1e-6
N_MOD = 6
LANES = 128
SSD_HEAD_DIM = 64
SSD_STATE = 128
SSD_CONV = 4
SSD_HEADS_PER_GROUP = 8
SSD_GROUP_WIDTH = SSD_HEADS_PER_GROUP * SSD_HEAD_DIM
SSD_CHUNK = 128
FOX_HEAD_DIM = 128
LOG2E = 1.4426950408889634
VMEM_LIMIT = 56 * 1024 * 1024


def _cparams(sem):
    return pltpu.CompilerParams(dimension_semantics=sem, vmem_limit_bytes=VMEM_LIMIT)


def _silu(x):
    return x * (1.0 / (1.0 + jnp.exp(-x)))


def _softplus(x):
    return jnp.maximum(x, 0.0) + jnp.log1p(jnp.exp(-jnp.abs(x)))


def _split3(x):
    hi = x.astype(BF16)
    r1 = x - hi.astype(F32)
    mid = r1.astype(BF16)
    lo = (r1 - mid.astype(F32)).astype(BF16)
    return hi, mid, lo


def _cumsum_rows(x):
    n = x.shape[0]
    row = lax.broadcasted_iota(jnp.int32, x.shape, 0)
    k = 1
    while k < n:
        x = x + jnp.where(row >= k, pltpu.roll(x, k, 0), 0.0)
        k *= 2
    return x


def _norm_mod(x, g, shift, scale):
    ms = jnp.mean(x * x, axis=-1, keepdims=True)
    y = x * lax.rsqrt(ms + EPS) * g
    return y * (1.0 + scale) + shift


def _ada_kernel(c_ref, w_ref, b_ref, o_ref):
    cond = _silu(c_ref[...])
    o_ref[0] = jnp.dot(cond.astype(BF16), w_ref[0].astype(BF16),
                       preferred_element_type=F32) + b_ref[0]


def _ada(c_pad, w_ada, b_ada, tn):
    depth, d, n = w_ada.shape
    rows = c_pad.shape[0]
    return pl.pallas_call(
        _ada_kernel,
        grid=(depth, n // tn),
        in_specs=[
            pl.BlockSpec((rows, d), lambda l, j: (0, 0)),
            pl.BlockSpec((1, d, tn), lambda l, j: (l, 0, j)),
            pl.BlockSpec((1, 1, tn), lambda l, j: (l, 0, j)),
        ],
        out_specs=pl.BlockSpec((1, rows, tn), lambda l, j: (l, 0, j)),
        out_shape=jax.ShapeDtypeStruct((depth, rows, n), F32),
        compiler_params=_cparams(("parallel", "parallel")),
        name="ada",
    )(c_pad, w_ada, b_ada.reshape(depth, 1, n))


def _in_proj_kernel(x_ref, g_ref, sh_ref, sc_ref, w_ref, wx_ref, o_ref, ox_ref, u_ref,
                    *, n_scaled, scale):
    j = pl.program_id(1)

    @pl.when(j == 0)
    def _():
        u = _norm_mod(x_ref[...], g_ref[...], sh_ref[0], sc_ref[0]).astype(BF16)
        u_ref[...] = u
        ox_ref[...] = jnp.dot(u, wx_ref[...], preferred_element_type=F32)

    acc = jnp.dot(u_ref[...], w_ref[...], preferred_element_type=F32)
    if n_scaled:
        acc = acc * jnp.where(j < n_scaled, scale, 1.0)
    o_ref[0] = acc.astype(o_ref.dtype)


def _in_proj(h, g, mod, sh_row, sc_row, w, wx, seq, tm, tn, n_scaled=0, scale=1.0):
    m, d = h.shape
    n = w.shape[1]
    per_b = seq // tm
    return pl.pallas_call(
        functools.partial(_in_proj_kernel, n_scaled=n_scaled, scale=scale),
        grid=(m // tm, n // tn),
        in_specs=[
            pl.BlockSpec((tm, d), lambda i, j: (i, 0)),
            pl.BlockSpec((1, d), lambda i, j: (0, 0)),
            pl.BlockSpec((1, 1, d), lambda i, j: (sh_row + i // per_b, 0, 0)),
            pl.BlockSpec((1, 1, d), lambda i, j: (sc_row + i // per_b, 0, 0)),
            pl.BlockSpec((d, tn), lambda i, j: (0, j)),
            pl.BlockSpec((d, LANES), lambda i, j: (0, 0)),
        ],
        out_specs=[
            pl.BlockSpec((1, tm, tn), lambda i, j: (j, i, 0)),
            pl.BlockSpec((tm, LANES), lambda i, j: (i, 0)),
        ],
        out_shape=[
            jax.ShapeDtypeStruct((n // tn, m, tn), BF16),
            jax.ShapeDtypeStruct((m, LANES), F32),
        ],
        scratch_shapes=[pltpu.VMEM((tm, d), BF16)],
        compiler_params=_cparams(("parallel", "arbitrary")),
        name="in_proj",
    )(h, g, mod, mod, w, wx)


def _out_proj_kernel(a_ref, w_ref, h_ref, gate_ref, o_ref):
    acc = jnp.dot(a_ref[...], w_ref[...], preferred_element_type=F32)
    o_ref[...] = h_ref[...] + gate_ref[0] * acc


def _out_proj(a, w, h, mod, gate_row, seq, tm, tn):
    m, k = a.shape
    d = w.shape[1]
    per_b = seq // tm
    return pl.pallas_call(
        _out_proj_kernel,
        grid=(m // tm, d // tn),
        in_specs=[
            pl.BlockSpec((tm, k), lambda i, j: (i, 0)),
            pl.BlockSpec((k, tn), lambda i, j: (0, j)),
            pl.BlockSpec((tm, tn), lambda i, j: (i, j)),
            pl.BlockSpec((1, 1, tn), lambda i, j: (gate_row + i // per_b, 0, j)),
        ],
        out_specs=pl.BlockSpec((tm, tn), lambda i, j: (i, j)),
        out_shape=jax.ShapeDtypeStruct((m, d), F32),
        compiler_params=_cparams(("parallel", "arbitrary")),
        name="out_proj",
    )(a, w, h, mod)


def _mlp_kernel(h_ref, g_ref, sh_ref, sc_ref, gate_ref, wu_ref, wd_ref, fn_ref, o_ref, u_ref,
                *, final_norm):
    f = pl.program_id(1)

    @pl.when(f == 0)
    def _():
        u_ref[...] = _norm_mod(h_ref[...], g_ref[...], sh_ref[0], sc_ref[0]).astype(BF16)
        o_ref[...] = jnp.zeros_like(o_ref)

    a = jnp.dot(u_ref[...], wu_ref[...], preferred_element_type=F32)
    a = jnp.square(jnp.maximum(a, 0.0)).astype(BF16)
    o_ref[...] += jnp.dot(a, wd_ref[...], preferred_element_type=F32)

    @pl.when(f == pl.num_programs(1) - 1)
    def _():
        y = h_ref[...] + gate_ref[0] * o_ref[...]
        if final_norm:
            ms = jnp.mean(y * y, axis=-1, keepdims=True)
            y = y * lax.rsqrt(ms + EPS) * fn_ref[...]
        o_ref[...] = y


def _mlp(h, g, mod, sh_row, sc_row, gate_row, w_up, w_down, fn, seq, tm, tf, final_norm):
    m, d = h.shape
    dff = w_up.shape[1]
    per_b = seq // tm
    return pl.pallas_call(
        functools.partial(_mlp_kernel, final_norm=final_norm),
        grid=(m // tm, dff // tf),
        in_specs=[
            pl.BlockSpec((tm, d), lambda i, f: (i, 0)),
            pl.BlockSpec((1, d), lambda i, f: (0, 0)),
            pl.BlockSpec((1, 1, d), lambda i, f: (sh_row + i // per_b, 0, 0)),
            pl.BlockSpec((1, 1, d), lambda i, f: (sc_row + i // per_b, 0, 0)),
            pl.BlockSpec((1, 1, d), lambda i, f: (gate_row + i // per_b, 0, 0)),
            pl.BlockSpec((d, tf), lambda i, f: (0, f)),
            pl.BlockSpec((tf, d), lambda i, f: (f, 0)),
            pl.BlockSpec((1, d), lambda i, f: (0, 0)),
        ],
        out_specs=pl.BlockSpec((tm, d), lambda i, f: (i, 0)),
        out_shape=jax.ShapeDtypeStruct((m, d), F32),
        scratch_shapes=[pltpu.VMEM((tm, d), BF16)],
        compiler_params=_cparams(("parallel", "arbitrary")),
        name="mlp",
    )(h, g, mod, mod, mod, w_up, w_down, fn)


def _ssd_kernel(z_ref, x_ref, b_ref, c_ref, dt_ref, wx_ref, wb_ref, wc_ref, bx_ref, bb_ref, bc_ref,
                dtb_ref, alog_ref, d_ref, gn_ref, o_ref, state_ref, xp_ref, bp_ref, cp_ref, *, t_blk):
    g = pl.program_id(1)
    t = pl.program_id(2)
    ck = SSD_CHUNK
    gw = SSD_GROUP_WIDTH

    @pl.when(t == 0)
    def _():
        state_ref[...] = jnp.zeros_like(state_ref)
        xp_ref[0:8, :] = jnp.zeros((8, gw), F32)
        bp_ref[0:8, :] = jnp.zeros((8, SSD_STATE), F32)
        cp_ref[0:8, :] = jnp.zeros((8, SSD_STATE), F32)

    xp_ref[8:8 + t_blk, :] = x_ref[0].astype(F32)
    bp_ref[8:8 + t_blk, :] = b_ref[0].astype(F32)
    cp_ref[8:8 + t_blk, :] = c_ref[0].astype(F32)

    def conv(pad_ref, w_ref, bias_ref, r0):
        acc = bias_ref[...] + w_ref[0:1, :] * pad_ref[pl.ds(r0 + 5, ck), :]
        for k in range(1, SSD_CONV):
            acc = acc + w_ref[k:k + 1, :] * pad_ref[pl.ds(r0 + 5 + k, ck), :]
        return _silu(acc)

    row = lax.broadcasted_iota(jnp.int32, (ck, ck), 0)
    col = lax.broadcasted_iota(jnp.int32, (ck, ck), 1)
    causal = row >= col
    low_half = col < SSD_HEAD_DIM
    e_h = lax.broadcasted_iota(jnp.int32, (LANES, gw), 0)
    e_c = lax.broadcasted_iota(jnp.int32, (LANES, gw), 1)
    expand = (e_h == g * SSD_HEADS_PER_GROUP + (e_c >> 6)).astype(BF16)
    s_j = lax.broadcasted_iota(jnp.int32, (16, LANES), 0)
    s_h = lax.broadcasted_iota(jnp.int32, (16, LANES), 1)
    select = ((s_h == g * SSD_HEADS_PER_GROUP + s_j) & (s_j < SSD_HEADS_PER_GROUP)).astype(BF16)

    a_neg = -jnp.exp(alog_ref[...])
    nt = (((1,), (1,)), ((), ()))
    tn = (((0,), (0,)), ((), ()))

    for c in range(t_blk // ck):
        r0 = c * ck
        xs = conv(xp_ref, wx_ref, bx_ref, r0)
        bm = conv(bp_ref, wb_ref, bb_ref, r0).astype(BF16)
        cm = conv(cp_ref, wc_ref, bc_ref, r0).astype(BF16)

        dt = _softplus(dt_ref[r0:r0 + ck, :] + dtb_ref[...])
        acum = _cumsum_rows(dt * a_neg)
        a_last = acum[ck - 1:ck, :]
        stack = jnp.concatenate(
            [dt, dt * jnp.exp(a_last - acum), jnp.exp(acum), acum,
             jnp.broadcast_to(jnp.exp(a_last), (8, LANES))], axis=0)
        s_hi, s_mid, s_lo = _split3(stack)
        ex = (jnp.dot(s_hi, expand, preferred_element_type=F32)
              + jnp.dot(s_mid, expand, preferred_element_type=F32)
              + jnp.dot(s_lo, expand, preferred_element_type=F32))
        dt_x = ex[0:ck]
        wend_x = ex[ck:2 * ck]
        ea_x = ex[2 * ck:3 * ck]
        acum_x = ex[3 * ck:4 * ck]
        cd_x = ex[4 * ck:4 * ck + 1]
        a_hi, a_mid, a_lo = _split3(acum)
        acum_t = (lax.dot_general(select, a_hi, nt, preferred_element_type=F32)
                  + lax.dot_general(select, a_mid, nt, preferred_element_type=F32)
                  + lax.dot_general(select, a_lo, nt, preferred_element_type=F32))

        scores = lax.dot_general(cm, bm, nt, preferred_element_type=F32)
        xdt = xs * dt_x
        y_pairs = []
        for p in range(SSD_HEADS_PER_GROUP // 2):
            x_pair = xdt[:, p * LANES:(p + 1) * LANES]
            halves = (jnp.where(low_half, x_pair, 0.0).astype(BF16),
                      jnp.where(low_half, 0.0, x_pair).astype(BF16))
            y_p = None
            for q in range(2):
                j = 2 * p + q
                a_col = jnp.broadcast_to(acum_x[:, j * SSD_HEAD_DIM:j * SSD_HEAD_DIM + 1], (ck, ck))
                seg = a_col - acum_t[j:j + 1, :]
                decay = jnp.exp(jnp.where(causal, seg, -jnp.inf))
                mh = (scores * decay).astype(BF16)
                y_q = jnp.dot(mh, halves[q], preferred_element_type=F32)
                y_p = y_q if y_p is None else y_p + y_q
            y_pairs.append(y_p)
        y = jnp.concatenate(y_pairs, axis=1)

        state = state_ref[...]
        y = y + jnp.dot(cm, state.astype(BF16), preferred_element_type=F32) * ea_x
        state_ref[...] = state * cd_x + lax.dot_general(
            bm, (xs * wend_x).astype(BF16), tn, preferred_element_type=F32)

        y = y + xs * d_ref[...]
        y = y * _silu(z_ref[0, r0:r0 + ck, :].astype(F32))
        ms = jnp.mean(y * y, axis=-1, keepdims=True)
        y = y * lax.rsqrt(ms + EPS) * gn_ref[...]
        o_ref[r0:r0 + ck, :] = y.astype(o_ref.dtype)

    xp_ref[0:8, :] = xp_ref[t_blk:t_blk + 8, :]
    bp_ref[0:8, :] = bp_ref[t_blk:t_blk + 8, :]
    cp_ref[0:8, :] = cp_ref[t_blk:t_blk + 8, :]


def _ssd(zxbc, dt_raw, conv_w, conv_b, dtb, alog, d_x, gnorm, batch, seq, t_blk, tn_in):
    m = dt_raw.shape[0]
    gw = SSD_GROUP_WIDTH
    d_inner = gnorm.shape[1]
    groups = d_inner // gw
    n_t = seq // t_blk
    per_tile = tn_in // gw
    x_tile0 = d_inner // tn_in
    b_tile = 2 * d_inner // tn_in
    c_tile = b_tile + 1
    assert groups * SSD_STATE == tn_in and tn_in % gw == 0
    xw0 = 0
    bw0 = d_inner // SSD_STATE
    cw0 = bw0 + groups

    def rows(b, t):
        return b * n_t + t

    return pl.pallas_call(
        functools.partial(_ssd_kernel, t_blk=t_blk),
        grid=(batch, groups, n_t),
        in_specs=[
            pl.BlockSpec((1, t_blk, gw), lambda b, g, t: (g // per_tile, rows(b, t), g % per_tile)),
            pl.BlockSpec((1, t_blk, gw), lambda b, g, t: (x_tile0 + g // per_tile, rows(b, t), g % per_tile)),
            pl.BlockSpec((1, t_blk, SSD_STATE), lambda b, g, t: (b_tile, rows(b, t), g)),
            pl.BlockSpec((1, t_blk, SSD_STATE), lambda b, g, t: (c_tile, rows(b, t), g)),
            pl.BlockSpec((t_blk, LANES), lambda b, g, t: (rows(b, t), 0)),
            pl.BlockSpec((SSD_CONV, gw), lambda b, g, t: (0, xw0 + g)),
            pl.BlockSpec((SSD_CONV, SSD_STATE), lambda b, g, t: (0, bw0 + g)),
            pl.BlockSpec((SSD_CONV, SSD_STATE), lambda b, g, t: (0, cw0 + g)),
            pl.BlockSpec((1, gw), lambda b, g, t: (0, xw0 + g)),
            pl.BlockSpec((1, SSD_STATE), lambda b, g, t: (0, bw0 + g)),
            pl.BlockSpec((1, SSD_STATE), lambda b, g, t: (0, cw0 + g)),
            pl.BlockSpec((1, LANES), lambda b, g, t: (0, 0)),
            pl.BlockSpec((1, LANES), lambda b, g, t: (0, 0)),
            pl.BlockSpec((1, gw), lambda b, g, t: (0, g)),
            pl.BlockSpec((1, gw), lambda b, g, t: (0, g)),
        ],
        out_specs=pl.BlockSpec((t_blk, gw), lambda b, g, t: (rows(b, t), g)),
        out_shape=jax.ShapeDtypeStruct((m, d_inner), BF16),
        scratch_shapes=[
            pltpu.VMEM((SSD_STATE, gw), F32),
            pltpu.VMEM((t_blk + 8, gw), F32),
            pltpu.VMEM((t_blk + 8, SSD_STATE), F32),
            pltpu.VMEM((t_blk + 8, SSD_STATE), F32),
        ],
        compiler_params=_cparams(("parallel", "parallel", "arbitrary")),
        name="ssd",
    )(zxbc, zxbc, zxbc, zxbc, dt_raw, conv_w, conv_w, conv_w, conv_b, conv_b, conv_b,
      dtb, alog, d_x, gnorm)


def _fcum_kernel(f_ref, b_ref, o_ref, ot_ref):
    x = f_ref[...] + b_ref[...]
    cum = _cumsum_rows(-_softplus(-x)) * LOG2E
    o_ref[0] = cum
    ot_ref[0] = jnp.transpose(cum)


def _fcum(f_logit, b_f, batch, seq):
    return pl.pallas_call(
        _fcum_kernel,
        grid=(batch,),
        in_specs=[
            pl.BlockSpec((seq, LANES), lambda b: (b, 0)),
            pl.BlockSpec((1, LANES), lambda b: (0, 0)),
        ],
        out_specs=[
            pl.BlockSpec((1, seq, LANES), lambda b: (b, 0, 0)),
            pl.BlockSpec((1, LANES, seq), lambda b: (b, 0, 0)),
        ],
        out_shape=[
            jax.ShapeDtypeStruct((batch, seq, LANES), F32),
            jax.ShapeDtypeStruct((batch, LANES, seq), F32),
        ],
        compiler_params=_cparams(("parallel",)),
        name="fcum",
    )(f_logit, b_f)


def _key_bias_columns(cum, head):
    r = lax.broadcasted_iota(jnp.int32, (LANES, LANES), 0)
    c = lax.broadcasted_iota(jnp.int32, (LANES, LANES), 1)
    out = None
    for i, piece in enumerate(_split3(cum)):
        sel = ((r == head) & (c == i)).astype(BF16)
        term = jnp.dot(piece, sel, preferred_element_type=F32)
        out = term if out is None else out + term
    return (-out).astype(BF16)


def _attn_kernel(q_ref, k_ref, v_ref, cq_ref, ck_ref, o_ref, kx_ref, vt_ref, s_ref, p_ref, acc_ref,
                 *, tq, tk, seq):
    head = pl.program_id(1)
    qi = pl.program_id(2)
    hd = FOX_HEAD_DIM
    nt = (((1,), (1,)), ((), ()))
    assert tq == 2 * tk

    @pl.when(qi == 0)
    def _():
        for rc in range(seq // tk):
            r0 = rc * tk
            kx_ref[rc, :, 0:hd] = k_ref[0, r0:r0 + tk, :]
            kx_ref[rc, :, hd:2 * hd] = _key_bias_columns(ck_ref[0, r0:r0 + tk, :], head)
            vt_ref[rc] = jnp.transpose(v_ref[0, r0:r0 + tk, :].astype(F32)).astype(BF16)

    lane = lax.broadcasted_iota(jnp.int32, (tq, hd), 1)
    qx = jnp.concatenate([q_ref[0], (lane < 3).astype(BF16)], axis=1)
    cq = cq_ref[0, pl.ds(head % 8, 1), :]

    def s1(t, slot):
        s_ref[slot] = lax.dot_general(kx_ref[t], qx, nt, preferred_element_type=F32)

    def s2(slot, m, l, key_offset=None):
        st = s_ref[slot]
        if key_offset is not None:
            k_idx = lax.broadcasted_iota(jnp.int32, st.shape, 0) + key_offset
            q_idx = lax.broadcasted_iota(jnp.int32, st.shape, 1)
            st = jnp.where(k_idx <= q_idx, st, -jnp.inf)
        m_new = jnp.maximum(m, jnp.max(st, axis=0, keepdims=True) + cq)
        alpha = jnp.exp2(m - m_new)
        p = jnp.exp2(st - (m_new - cq))
        p_ref[slot] = p.astype(BF16)
        return m_new, alpha * l + jnp.sum(p, axis=0, keepdims=True), alpha

    def s3(t, slot, alpha):
        acc_ref[...] = alpha * acc_ref[...] + jnp.dot(vt_ref[t], p_ref[slot],
                                                      preferred_element_type=F32)

    acc_ref[...] = jnp.zeros_like(acc_ref)
    p_ref[1] = jnp.zeros(p_ref.shape[1:], BF16)
    s1(0, 0)

    def body(j, carry):
        m, l, alpha = carry
        t = 2 * j
        s3(jnp.maximum(t - 1, 0), 1, alpha)
        s1(t + 1, 1)
        m, l, alpha = s2(0, m, l)
        s3(t, 0, alpha)
        s1(t + 2, 0)
        return s2(1, m, l)

    m0 = jnp.full((1, tq), -jnp.inf, F32)
    l0 = jnp.zeros((1, tq), F32)
    m, l, alpha = lax.fori_loop(0, qi, body, (m0, l0, jnp.ones((1, tq), F32)))
    t = 2 * qi
    s3(jnp.maximum(t - 1, 0), 1, alpha)
    s1(t + 1, 1)
    m, l, alpha = s2(0, m, l, key_offset=0)
    s3(t, 0, alpha)
    m, l, alpha = s2(1, m, l, key_offset=tk)
    s3(t + 1, 1, alpha)
    o_ref[...] = jnp.transpose(acc_ref[...] / l).astype(o_ref.dtype)


def _attn(qkv, cum, cum_t, batch, seq, heads, tq, tn_in):
    m = qkv.shape[1]
    hd = FOX_HEAD_DIM
    per_tile = tn_in // hd
    tiles = heads // per_tile
    n_q = seq // tq
    tk = tq // 2
    return pl.pallas_call(
        functools.partial(_attn_kernel, tq=tq, tk=tk, seq=seq),
        grid=(batch, heads, n_q),
        in_specs=[
            pl.BlockSpec((1, tq, hd), lambda b, h, i: (h // per_tile, b * n_q + i, h % per_tile)),
            pl.BlockSpec((1, seq, hd), lambda b, h, i: (tiles + h // per_tile, b, h % per_tile)),
            pl.BlockSpec((1, seq, hd), lambda b, h, i: (2 * tiles + h // per_tile, b, h % per_tile)),
            pl.BlockSpec((1, 8, tq), lambda b, h, i: (b, h // 8, i)),
            pl.BlockSpec((1, seq, LANES), lambda b, h, i: (b, 0, 0)),
        ],
        out_specs=pl.BlockSpec((tq, hd), lambda b, h, i: (b * n_q + i, h)),
        out_shape=jax.ShapeDtypeStruct((m, heads * hd), BF16),
        scratch_shapes=[
            pltpu.VMEM((seq // tk, tk, 2 * hd), BF16),
            pltpu.VMEM((seq // tk, hd, tk), BF16),
            pltpu.VMEM((2, tk, tq), F32),
            pltpu.VMEM((2, tk, tq), BF16),
            pltpu.VMEM((hd, tq), F32),
        ],
        compiler_params=_cparams(("parallel", "parallel", "arbitrary")),
        name="attn",
    )(qkv, qkv, qkv, cum_t, cum)


def _pad_cols(a, n):
    return jnp.pad(a, ((0, 0), (0, n - a.shape[1])))


def kernel(x, c, norm_mix, norm_mlp, w_ada, b_ada, w_up, w_down, ssd_w_in, ssd_conv_w, ssd_conv_b,
           ssd_dt_bias, ssd_A_log, ssd_D, ssd_gnorm, ssd_w_out, fox_w_in, fox_b_f, fox_w_out,
           final_norm):
    batch, seq, d = x.shape
    m = batch * seq
    depth = w_ada.shape[0]
    d_inner = ssd_gnorm.shape[1]
    ssd_heads = ssd_dt_bias.shape[1]
    fox_heads = fox_b_f.shape[1]
    fox_width = fox_heads * FOX_HEAD_DIM
    assert d_inner == ssd_heads * SSD_HEAD_DIM and ssd_heads <= LANES and fox_heads <= LANES

    tm = min(1024, seq)
    tn_in = (d_inner // SSD_GROUP_WIDTH) * SSD_STATE
    tn_out = min(512, d)
    tf = min(512, w_up.shape[2])
    t_blk = min(256, seq)
    tq = min(512, seq)
    assert fox_width % tn_in == 0 and d_inner % tn_in == 0

    rows = 8
    c_pad = jnp.pad(c, ((0, rows - batch), (0, 0)))
    mod = _ada(c_pad, w_ada, b_ada, tn=min(1024, N_MOD * d))[:, :batch]
    mod = mod.reshape(depth, batch, N_MOD, d).transpose(0, 2, 1, 3).reshape(depth * N_MOD * batch, 1, d)

    def mod_row(layer, which):
        return (layer * N_MOD + which) * batch

    h = x.reshape(m, d)
    for layer in range(depth):
        j = layer // 2
        g_mix = norm_mix[layer].reshape(1, d)
        g_mlp = norm_mlp[layer].reshape(1, d)
        if layer % 2 == 0:
            n_main = 2 * d_inner + 2 * (d_inner // SSD_GROUP_WIDTH) * SSD_STATE
            w_main = ssd_w_in[j][:, :n_main].astype(BF16)
            w_side = _pad_cols(ssd_w_in[j][:, n_main:], LANES).astype(BF16)
            zxbc, dt_raw = _in_proj(h, g_mix, mod, mod_row(layer, 0), mod_row(layer, 1),
                                    w_main, w_side, seq, tm, tn_in)
            y = _ssd(zxbc, dt_raw, ssd_conv_w[j], ssd_conv_b[j].reshape(1, -1),
                     _pad_cols(ssd_dt_bias[j].reshape(1, -1), LANES),
                     _pad_cols(ssd_A_log[j].reshape(1, -1), LANES),
                     jnp.repeat(ssd_D[j], SSD_HEAD_DIM).reshape(1, d_inner),
                     ssd_gnorm[j].reshape(1, d_inner), batch, seq, t_blk, tn_in)
            h = _out_proj(y, ssd_w_out[j].astype(BF16), h, mod, mod_row(layer, 2), seq, tm, tn_out)
        else:
            w_main = fox_w_in[j][:, :3 * fox_width].astype(BF16)
            w_side = _pad_cols(fox_w_in[j][:, 3 * fox_width:], LANES).astype(BF16)
            qkv, f_logit = _in_proj(h, g_mix, mod, mod_row(layer, 0), mod_row(layer, 1),
                                    w_main, w_side, seq, tm, tn_in,
                                    n_scaled=fox_width // tn_in, scale=FOX_HEAD_DIM ** -0.5 * LOG2E)
            cum, cum_t = _fcum(f_logit, _pad_cols(fox_b_f[j].reshape(1, -1), LANES), batch, seq)
            o = _attn(qkv, cum, cum_t, batch, seq, fox_heads, tq, tn_in)
            h = _out_proj(o, fox_w_out[j].astype(BF16), h, mod, mod_row(layer, 2), seq, tm, tn_out)
        last = layer == depth - 1
        h = _mlp(h, g_mlp, mod, mod_row(layer, 3), mod_row(layer, 4), mod_row(layer, 5),
                 w_up[layer].astype(BF16), w_down[layer].astype(BF16), final_norm.reshape(1, d),
                 seq, tm, tf, final_norm=last)
    return h.reshape(batch, seq, d)
```

```python
import functools

import jax
import jax.numpy as jnp
from jax import lax
from jax.experimental import pallas as pl
from jax.experimental.pallas import tpu as pltpu

F32 = jnp.float32
BF16 = jnp.bfloat16

EPS = 1e-6
N_MOD = 6
LANES = 128
SSD_HEAD_DIM = 64
SSD_STATE = 128
SSD_CONV = 4
SSD_HEADS_PER_GROUP = 8
SSD_GROUP_WIDTH = SSD_HEADS_PER_GROUP * SSD_HEAD_DIM
SSD_CHUNK = 128
FOX_HEAD_DIM = 128
LOG2E = 1.4426950408889634
VMEM_LIMIT = 56 * 1024 * 1024


def _cparams(sem):
    return pltpu.CompilerParams(dimension_semantics=sem, vmem_limit_bytes=VMEM_LIMIT)


def _silu(x):
    half = 0.5 * x
    return half + half * jnp.tanh(half)


def _softplus(x):
    return jnp.maximum(x, 0.0) + jnp.log1p(jnp.exp(-jnp.abs(x)))


def _split3(x):
    hi = x.astype(BF16)
    r1 = x - hi.astype(F32)
    mid = r1.astype(BF16)
    lo = (r1 - mid.astype(F32)).astype(BF16)
    return hi, mid, lo


def _cumsum_rows(x):
    n = x.shape[0]
    row = lax.broadcasted_iota(jnp.int32, x.shape, 0)
    k = 1
    while k < n:
        x = x + jnp.where(row >= k, pltpu.roll(x, k, 0), 0.0)
        k *= 2
    return x


def _norm_mod(x, g, shift, scale):
    ms = jnp.mean(x * x, axis=-1, keepdims=True)
    y = x * lax.rsqrt(ms + EPS) * g
    return y * (1.0 + scale) + shift


def _ada_kernel(c_ref, w_ref, b_ref, o_ref):
    cond = _silu(c_ref[...])
    o_ref[0] = jnp.dot(cond.astype(BF16), w_ref[0].astype(BF16),
                       preferred_element_type=F32) + b_ref[0]


def _ada(c_pad, w_ada, b_ada, tn):
    depth, d, n = w_ada.shape
    rows = c_pad.shape[0]
    return pl.pallas_call(
        _ada_kernel,
        grid=(depth, n // tn),
        in_specs=[
            pl.BlockSpec((rows, d), lambda l, j: (0, 0)),
            pl.BlockSpec((1, d, tn), lambda l, j: (l, 0, j)),
            pl.BlockSpec((1, 1, tn), lambda l, j: (l, 0, j)),
        ],
        out_specs=pl.BlockSpec((1, rows, tn), lambda l, j: (l, 0, j)),
        out_shape=jax.ShapeDtypeStruct((depth, rows, n), F32),
        compiler_params=_cparams(("parallel", "parallel")),
        name="ada",
    )(c_pad, w_ada, b_ada.reshape(depth, 1, n))


def _in_proj_kernel(x_ref, g_ref, sh_ref, sc_ref, w_ref, wx_ref, o_ref, ox_ref, u_ref,
                    *, n_scaled, scale):
    j = pl.program_id(1)

    @pl.when(j == 0)
    def _():
        u = _norm_mod(x_ref[...], g_ref[...], sh_ref[0], sc_ref[0]).astype(BF16)
        u_ref[...] = u
        ox_ref[...] = jnp.dot(u, wx_ref[...].astype(BF16), preferred_element_type=F32)

    acc = jnp.dot(u_ref[...], w_ref[0].astype(BF16), preferred_element_type=F32)
    if n_scaled:
        acc = acc * jnp.where(j < n_scaled, scale, 1.0)
    o_ref[0] = acc.astype(o_ref.dtype)


def _in_proj(h, g, mod, sh_row, sc_row, w, w_layer, n, wx, seq, tm, tn, n_scaled=0, scale=1.0):
    m, d = h.shape
    per_b = seq // tm
    return pl.pallas_call(
        functools.partial(_in_proj_kernel, n_scaled=n_scaled, scale=scale),
        grid=(m // tm, n // tn),
        in_specs=[
            pl.BlockSpec((tm, d), lambda i, j: (i, 0)),
            pl.BlockSpec((1, d), lambda i, j: (0, 0)),
            pl.BlockSpec((1, 1, d), lambda i, j: (sh_row + i // per_b, 0, 0)),
            pl.BlockSpec((1, 1, d), lambda i, j: (sc_row + i // per_b, 0, 0)),
            pl.BlockSpec((1, d, tn), lambda i, j: (w_layer, 0, j)),
            pl.BlockSpec((d, LANES), lambda i, j: (0, 0)),
        ],
        out_specs=[
            pl.BlockSpec((1, tm, tn), lambda i, j: (j, i, 0)),
            pl.BlockSpec((tm, LANES), lambda i, j: (i, 0)),
        ],
        out_shape=[
            jax.ShapeDtypeStruct((n // tn, m, tn), BF16),
            jax.ShapeDtypeStruct((m, LANES), F32),
        ],
        scratch_shapes=[pltpu.VMEM((tm, d), BF16)],
        compiler_params=_cparams(("parallel", "arbitrary")),
        name="in_proj",
    )(h, g, mod, mod, w, wx)


def _out_proj_kernel(a_ref, w_ref, h_ref, gate_ref, o_ref):
    acc = jnp.dot(a_ref[...], w_ref[0].astype(BF16), preferred_element_type=F32)
    o_ref[...] = h_ref[...] + gate_ref[0] * acc


def _out_proj(a, w, w_layer, h, mod, gate_row, seq, tm, tn):
    m, k = a.shape
    d = w.shape[2]
    per_b = seq // tm
    return pl.pallas_call(
        _out_proj_kernel,
        grid=(m // tm, d // tn),
        in_specs=[
            pl.BlockSpec((tm, k), lambda i, j: (i, 0)),
            pl.BlockSpec((1, k, tn), lambda i, j: (w_layer, 0, j)),
            pl.BlockSpec((tm, tn), lambda i, j: (i, j)),
            pl.BlockSpec((1, 1, tn), lambda i, j: (gate_row + i // per_b, 0, j)),
        ],
        out_specs=pl.BlockSpec((tm, tn), lambda i, j: (i, j)),
        out_shape=jax.ShapeDtypeStruct((m, d), F32),
        compiler_params=_cparams(("parallel", "arbitrary")),
        name="out_proj",
    )(a, w, h, mod)


def _mlp_kernel(h_ref, g_ref, sh_ref, sc_ref, gate_ref, wu_ref, wd_ref, fn_ref, o_ref, u_ref,
                *, final_norm):
    f = pl.program_id(1)

    @pl.when(f == 0)
    def _():
        u_ref[...] = _norm_mod(h_ref[...], g_ref[...], sh_ref[0], sc_ref[0]).astype(BF16)
        o_ref[...] = jnp.zeros_like(o_ref)

    a = jnp.dot(u_ref[...], wu_ref[0].astype(BF16), preferred_element_type=F32)
    a = jnp.square(jnp.maximum(a, 0.0)).astype(BF16)
    o_ref[...] += jnp.dot(a, wd_ref[0].astype(BF16), preferred_element_type=F32)

    @pl.when(f == pl.num_programs(1) - 1)
    def _():
        y = h_ref[...] + gate_ref[0] * o_ref[...]
        if final_norm:
            ms = jnp.mean(y * y, axis=-1, keepdims=True)
            y = y * lax.rsqrt(ms + EPS) * fn_ref[...]
        o_ref[...] = y


def _mlp(h, g, mod, sh_row, sc_row, gate_row, w_up, w_down, layer, fn, seq, tm, tf, final_norm):
    m, d = h.shape
    dff = w_up.shape[2]
    per_b = seq // tm
    return pl.pallas_call(
        functools.partial(_mlp_kernel, final_norm=final_norm),
        grid=(m // tm, dff // tf),
        in_specs=[
            pl.BlockSpec((tm, d), lambda i, f: (i, 0), pipeline_mode=pl.Buffered(1)),
            pl.BlockSpec((1, d), lambda i, f: (0, 0)),
            pl.BlockSpec((1, 1, d), lambda i, f: (sh_row + i // per_b, 0, 0)),
            pl.BlockSpec((1, 1, d), lambda i, f: (sc_row + i // per_b, 0, 0)),
            pl.BlockSpec((1, 1, d), lambda i, f: (gate_row + i // per_b, 0, 0)),
            pl.BlockSpec((1, d, tf), lambda i, f: (layer, 0, f)),
            pl.BlockSpec((1, tf, d), lambda i, f: (layer, f, 0)),
            pl.BlockSpec((1, d), lambda i, f: (0, 0)),
        ],
        out_specs=pl.BlockSpec((tm, d), lambda i, f: (i, 0)),
        out_shape=jax.ShapeDtypeStruct((m, d), F32),
        scratch_shapes=[pltpu.VMEM((tm, d), BF16)],
        compiler_params=_cparams(("parallel", "arbitrary")),
        name="mlp",
    )(h, g, mod, mod, mod, w_up, w_down, fn)


def _ssd_kernel(z_ref, x_ref, b_ref, c_ref, dt_ref, wx_ref, wb_ref, wc_ref, bx_ref, bb_ref, bc_ref,
                dtb_ref, alog_ref, d_ref, gn_ref, o_ref, state_ref, xbc_ref, *, t_blk):
    g = pl.program_id(1)
    t = pl.program_id(2)
    ck = SSD_CHUNK
    gw = SSD_GROUP_WIDTH
    cw = gw + 2 * SSD_STATE

    @pl.when(t == 0)
    def _():
        state_ref[...] = jnp.zeros_like(state_ref)
        xbc_ref[0:ck, :] = jnp.zeros((ck, cw), BF16)

    xbc_ref[ck:ck + t_blk, 0:gw] = x_ref[0]
    xbc_ref[ck:ck + t_blk, gw:gw + SSD_STATE] = b_ref[0]
    xbc_ref[ck:ck + t_blk, gw + SSD_STATE:cw] = c_ref[0]
    conv_w = jnp.concatenate([wx_ref[...], wb_ref[...], wc_ref[...]], axis=1)
    conv_b = jnp.concatenate([bx_ref[...], bb_ref[...], bc_ref[...]], axis=1)

    sh_r = lax.broadcasted_iota(jnp.int32, ((SSD_CONV - 1) * ck, 2 * ck), 0)
    sh_c = lax.broadcasted_iota(jnp.int32, ((SSD_CONV - 1) * ck, 2 * ck), 1)
    shift = (sh_c == (sh_r & (ck - 1)) + (ck - (SSD_CONV - 1)) + (sh_r >> 7)).astype(BF16)
    assert ck == 128

    row = lax.broadcasted_iota(jnp.int32, (ck, ck), 0)
    col = lax.broadcasted_iota(jnp.int32, (ck, ck), 1)
    causal = row >= col
    low_half = col < SSD_HEAD_DIM
    e_h = lax.broadcasted_iota(jnp.int32, (LANES, gw), 0)
    e_c = lax.broadcasted_iota(jnp.int32, (LANES, gw), 1)
    expand = (e_h == g * SSD_HEADS_PER_GROUP + (e_c >> 6)).astype(BF16)
    expand2 = jnp.concatenate([expand, expand], axis=0)
    expand3 = jnp.concatenate([expand2, expand], axis=0)
    to_lane0 = (LANES - g * SSD_HEADS_PER_GROUP) % LANES

    a_neg = -jnp.exp(alog_ref[...])
    nt = (((1,), (1,)), ((), ()))
    tn = (((0,), (0,)), ((), ()))

    def chunk(c, carry):
        r0 = pl.multiple_of(c * ck, ck)
        window = xbc_ref[pl.ds(r0, 2 * ck), :]
        back = jnp.dot(shift, window, preferred_element_type=F32)
        acc = conv_b + conv_w[SSD_CONV - 1:SSD_CONV, :] * window[ck:2 * ck].astype(F32)
        for k in range(SSD_CONV - 1):
            acc = acc + conv_w[k:k + 1, :] * back[k * ck:(k + 1) * ck]
        conv = _silu(acc)
        xs = conv[:, 0:gw]
        bm = conv[:, gw:gw + SSD_STATE].astype(BF16)
        cm = conv[:, gw + SSD_STATE:cw].astype(BF16)

        dt = _softplus(dt_ref[pl.ds(r0, ck), :] + dtb_ref[...])
        acum = _cumsum_rows(dt * a_neg)
        a_last = acum[ck - 1:ck, :]
        scales = jnp.concatenate([dt, dt * jnp.exp(a_last - acum), jnp.exp(acum)], axis=0)
        ex = jnp.dot(jnp.concatenate(_split3(scales)[:2], axis=1), expand2,
                     preferred_element_type=F32)
        dt_x = ex[0:ck]
        wend_x = ex[ck:2 * ck]
        ea_x = ex[2 * ck:3 * ck]
        cd_x = jnp.dot(jnp.concatenate(_split3(jnp.broadcast_to(jnp.exp(a_last), (8, LANES))), axis=1),
                       expand3, preferred_element_type=F32)[0:1]
        acum_g = pltpu.roll(acum, to_lane0, 1)
        acum_t = jnp.transpose(acum_g)

        scores = lax.dot_general(cm, bm, nt, preferred_element_type=F32)
        xdt = (xs * dt_x).astype(BF16)
        y_pairs = []
        for p in range(SSD_HEADS_PER_GROUP // 2):
            x_pair = xdt[:, p * LANES:(p + 1) * LANES]
            rhs = jnp.concatenate([jnp.where(low_half, x_pair, jnp.zeros_like(x_pair)),
                                   jnp.where(low_half, jnp.zeros_like(x_pair), x_pair)], axis=0)
            lhs = []
            for j in (2 * p, 2 * p + 1):
                a_col = jnp.broadcast_to(acum_g[:, j:j + 1], (ck, ck))
                seg = a_col - acum_t[j:j + 1, :]
                decay = jnp.exp(jnp.where(causal, seg, -jnp.inf))
                lhs.append((scores * decay).astype(BF16))
            y_pairs.append(jnp.dot(jnp.concatenate(lhs, axis=1), rhs, preferred_element_type=F32))
        y = jnp.concatenate(y_pairs, axis=1)

        state = state_ref[...]
        y = y + jnp.dot(cm, state.astype(BF16), preferred_element_type=F32) * ea_x
        state_ref[...] = state * cd_x + lax.dot_general(
            bm, (xs * wend_x).astype(BF16), tn, preferred_element_type=F32)

        y = y + xs * d_ref[...]
        y = y * _silu(z_ref[0, pl.ds(r0, ck), :].astype(F32))
        ms = jnp.mean(y * y, axis=-1, keepdims=True)
        y = y * lax.rsqrt(ms + EPS) * gn_ref[...]
        o_ref[pl.ds(r0, ck), :] = y.astype(o_ref.dtype)
        return carry

    lax.fori_loop(0, t_blk // ck, chunk, 0, unroll=2)
    xbc_ref[0:ck, :] = xbc_ref[t_blk:t_blk + ck, :]


def _ssd(zxbc, dt_raw, conv_w, conv_b, dtb, alog, d_x, gnorm, batch, seq, t_blk, tn_in):
    m = dt_raw.shape[0]
    gw = SSD_GROUP_WIDTH
    d_inner = gnorm.shape[1]
    groups = d_inner // gw
    n_t = seq // t_blk
    per_tile = tn_in // gw
    x_tile0 = d_inner // tn_in
    b_tile = 2 * d_inner // tn_in
    c_tile = b_tile + 1
    assert groups * SSD_STATE == tn_in and tn_in % gw == 0
    xw0 = 0
    bw0 = d_inner // SSD_STATE
    cw0 = bw0 + groups

    def rows(b, t):
        return b * n_t + t

    return pl.pallas_call(
        functools.partial(_ssd_kernel, t_blk=t_blk),
        grid=(batch, groups, n_t),
        in_specs=[
            pl.BlockSpec((1, t_blk, gw), lambda b, g, t: (g // per_tile, rows(b, t), g % per_tile)),
            pl.BlockSpec((1, t_blk, gw), lambda b, g, t: (x_tile0 + g // per_tile, rows(b, t), g % per_tile)),
            pl.BlockSpec((1, t_blk, SSD_STATE), lambda b, g, t: (b_tile, rows(b, t), g)),
            pl.BlockSpec((1, t_blk, SSD_STATE), lambda b, g, t: (c_tile, rows(b, t), g)),
            pl.BlockSpec((t_blk, LANES), lambda b, g, t: (rows(b, t), 0)),
            pl.BlockSpec((SSD_CONV, gw), lambda b, g, t: (0, xw0 + g)),
            pl.BlockSpec((SSD_CONV, SSD_STATE), lambda b, g, t: (0, bw0 + g)),
            pl.BlockSpec((SSD_CONV, SSD_STATE), lambda b, g, t: (0, cw0 + g)),
            pl.BlockSpec((1, gw), lambda b, g, t: (0, xw0 + g)),
            pl.BlockSpec((1, SSD_STATE), lambda b, g, t: (0, bw0 + g)),
            pl.BlockSpec((1, SSD_STATE), lambda b, g, t: (0, cw0 + g)),
            pl.BlockSpec((1, LANES), lambda b, g, t: (0, 0)),
            pl.BlockSpec((1, LANES), lambda b, g, t: (0, 0)),
            pl.BlockSpec((1, gw), lambda b, g, t: (0, g)),
            pl.BlockSpec((1, gw), lambda b, g, t: (0, g)),
        ],
        out_specs=pl.BlockSpec((t_blk, gw), lambda b, g, t: (rows(b, t), g)),
        out_shape=jax.ShapeDtypeStruct((m, d_inner), BF16),
        scratch_shapes=[
            pltpu.VMEM((SSD_STATE, gw), F32),
            pltpu.VMEM((t_blk + SSD_CHUNK, gw + 2 * SSD_STATE), BF16),
        ],
        compiler_params=_cparams(("parallel", "parallel", "arbitrary")),
        name="ssd",
    )(zxbc, zxbc, zxbc, zxbc, dt_raw, conv_w, conv_w, conv_w, conv_b, conv_b, conv_b,
      dtb, alog, d_x, gnorm)


def _fcum_kernel(f_ref, b_ref, o_ref, ot_ref):
    x = f_ref[...] + b_ref[...]
    cum = _cumsum_rows(-_softplus(-x)) * LOG2E
    o_ref[0] = cum
    ot_ref[0] = jnp.transpose(cum)


def _fcum(f_logit, b_f, batch, seq):
    return pl.pallas_call(
        _fcum_kernel,
        grid=(batch,),
        in_specs=[
            pl.BlockSpec((seq, LANES), lambda b: (b, 0)),
            pl.BlockSpec((1, LANES), lambda b: (0, 0)),
        ],
        out_specs=[
            pl.BlockSpec((1, seq, LANES), lambda b: (b, 0, 0)),
            pl.BlockSpec((1, LANES, seq), lambda b: (b, 0, 0)),
        ],
        out_shape=[
            jax.ShapeDtypeStruct((batch, seq, LANES), F32),
            jax.ShapeDtypeStruct((batch, LANES, seq), F32),
        ],
        compiler_params=_cparams(("parallel",)),
        name="fcum",
    )(f_logit, b_f)


def _key_bias_columns(cum, head):
    r = lax.broadcasted_iota(jnp.int32, (LANES, LANES), 0)
    c = lax.broadcasted_iota(jnp.int32, (LANES, LANES), 1)
    out = None
    for i, piece in enumerate(_split3(cum)):
        sel = ((r == head) & (c == i)).astype(BF16)
        term = jnp.dot(piece, sel, preferred_element_type=F32)
        out = term if out is None else out + term
    return (-out).astype(BF16)


def _attn_kernel(q_ref, k_ref, v_ref, cq_ref, ck_ref, o_ref, kx_ref, vt_ref, s_ref, p_ref, acc_ref,
                 *, tq, tk, seq):
    head = pl.program_id(1)
    qi = pl.program_id(2)
    hd = FOX_HEAD_DIM
    nt = (((1,), (1,)), ((), ()))
    assert tq == 2 * tk

    @pl.when(qi == 0)
    def _():
        for rc in range(seq // tk):
            r0 = rc * tk
            kx_ref[rc, :, 0:hd] = k_ref[0, r0:r0 + tk, :]
            kx_ref[rc, :, hd:2 * hd] = _key_bias_columns(ck_ref[0, r0:r0 + tk, :], head)
            vt_ref[rc] = jnp.transpose(v_ref[0, r0:r0 + tk, :].astype(F32)).astype(BF16)

    lane = lax.broadcasted_iota(jnp.int32, (tq, hd), 1)
    qx = jnp.concatenate([q_ref[0], (lane < 3).astype(BF16)], axis=1)
    cq = cq_ref[0, pl.ds(head % 8, 1), :]

    def s1(t, slot):
        s_ref[slot] = lax.dot_general(kx_ref[t], qx, nt, preferred_element_type=F32)

    def s2(slot, m, l, key_offset=None):
        st = s_ref[slot]
        if key_offset is not None:
            k_idx = lax.broadcasted_iota(jnp.int32, st.shape, 0) + key_offset
            q_idx = lax.broadcasted_iota(jnp.int32, st.shape, 1)
            st = jnp.where(k_idx <= q_idx, st, -jnp.inf)
        m_new = jnp.maximum(m, jnp.max(st, axis=0, keepdims=True) + cq)
        alpha = jnp.exp2(m - m_new)
        p = jnp.exp2(st - (m_new - cq))
        p_ref[slot] = p.astype(BF16)
        return m_new, alpha * l + jnp.sum(p, axis=0, keepdims=True), alpha

    def s3(t, slot, alpha):
        acc_ref[...] = alpha * acc_ref[...] + jnp.dot(vt_ref[t], p_ref[slot],
                                                      preferred_element_type=F32)

    acc_ref[...] = jnp.zeros_like(acc_ref)
    p_ref[1] = jnp.zeros(p_ref.shape[1:], BF16)
    s1(0, 0)

    def body(j, carry):
        m, l, alpha = carry
        t = 2 * j
        s3(jnp.maximum(t - 1, 0), 1, alpha)
        s1(t + 1, 1)
        m, l, alpha = s2(0, m, l)
        s3(t, 0, alpha)
        s1(t + 2, 0)
        return s2(1, m, l)

    m0 = jnp.full((1, tq), -jnp.inf, F32)
    l0 = jnp.zeros((1, tq), F32)
    m, l, alpha = lax.fori_loop(0, qi, body, (m0, l0, jnp.ones((1, tq), F32)))
    t = 2 * qi
    s3(jnp.maximum(t - 1, 0), 1, alpha)
    s1(t + 1, 1)
    m, l, alpha = s2(0, m, l, key_offset=0)
    s3(t, 0, alpha)
    m, l, alpha = s2(1, m, l, key_offset=tk)
    s3(t + 1, 1, alpha)
    o_ref[...] = jnp.transpose(acc_ref[...] / l).astype(o_ref.dtype)


def _attn(qkv, cum, cum_t, batch, seq, heads, tq, tn_in):
    m = qkv.shape[1]
    hd = FOX_HEAD_DIM
    per_tile = tn_in // hd
    tiles = heads // per_tile
    n_q = seq // tq
    tk = tq // 2
    return pl.pallas_call(
        functools.partial(_attn_kernel, tq=tq, tk=tk, seq=seq),
        grid=(batch, heads, n_q),
        in_specs=[
            pl.BlockSpec((1, tq, hd), lambda b, h, i: (h // per_tile, b * n_q + i, h % per_tile)),
            pl.BlockSpec((1, seq, hd), lambda b, h, i: (tiles + h // per_tile, b, h % per_tile)),
            pl.BlockSpec((1, seq, hd), lambda b, h, i: (2 * tiles + h // per_tile, b, h % per_tile)),
            pl.BlockSpec((1, 8, tq), lambda b, h, i: (b, h // 8, i)),
            pl.BlockSpec((1, seq, LANES), lambda b, h, i: (b, 0, 0)),
        ],
        out_specs=pl.BlockSpec((tq, hd), lambda b, h, i: (b * n_q + i, h)),
        out_shape=jax.ShapeDtypeStruct((m, heads * hd), BF16),
        scratch_shapes=[
            pltpu.VMEM((seq // tk, tk, 2 * hd), BF16),
            pltpu.VMEM((seq // tk, hd, tk), BF16),
            pltpu.VMEM((2, tk, tq), F32),
            pltpu.VMEM((2, tk, tq), BF16),
            pltpu.VMEM((hd, tq), F32),
        ],
        compiler_params=_cparams(("parallel", "parallel", "arbitrary")),
        name="attn",
    )(qkv, qkv, qkv, cum_t, cum)


def _pad_cols(a, n):
    return jnp.pad(a, ((0, 0), (0, n - a.shape[1])))


def kernel(x, c, norm_mix, norm_mlp, w_ada, b_ada, w_up, w_down, ssd_w_in, ssd_conv_w, ssd_conv_b,
           ssd_dt_bias, ssd_A_log, ssd_D, ssd_gnorm, ssd_w_out, fox_w_in, fox_b_f, fox_w_out,
           final_norm):
    batch, seq, d = x.shape
    m = batch * seq
    depth = w_ada.shape[0]
    d_inner = ssd_gnorm.shape[1]
    ssd_heads = ssd_dt_bias.shape[1]
    fox_heads = fox_b_f.shape[1]
    fox_width = fox_heads * FOX_HEAD_DIM
    assert d_inner == ssd_heads * SSD_HEAD_DIM and ssd_heads <= LANES and fox_heads <= LANES

    tm = min(1024, seq)
    tn_in = (d_inner // SSD_GROUP_WIDTH) * SSD_STATE
    tn_out = min(512, d)
    tf = min(512, w_up.shape[2])
    t_blk = min(1024, seq)
    tq = min(512, seq)
    assert fox_width % tn_in == 0 and d_inner % tn_in == 0

    rows = 8
    c_pad = jnp.pad(c, ((0, rows - batch), (0, 0)))
    mod = _ada(c_pad, w_ada, b_ada, tn=min(1024, N_MOD * d))[:, :batch]
    mod = mod.reshape(depth, batch, N_MOD, d).transpose(0, 2, 1, 3).reshape(depth * N_MOD * batch, 1, d)

    def mod_row(layer, which):
        return (layer * N_MOD + which) * batch

    h = x.reshape(m, d)
    for layer in range(depth):
        j = layer // 2
        g_mix = norm_mix[layer].reshape(1, d)
        g_mlp = norm_mlp[layer].reshape(1, d)
        if layer % 2 == 0:
            n_main = 2 * d_inner + 2 * (d_inner // SSD_GROUP_WIDTH) * SSD_STATE
            w_side = _pad_cols(ssd_w_in[j][:, n_main:], LANES)
            zxbc, dt_raw = _in_proj(h, g_mix, mod, mod_row(layer, 0), mod_row(layer, 1),
                                    ssd_w_in, j, n_main, w_side, seq, tm, tn_in)
            y = _ssd(zxbc, dt_raw, ssd_conv_w[j], ssd_conv_b[j].reshape(1, -1),
                     _pad_cols(ssd_dt_bias[j].reshape(1, -1), LANES),
                     _pad_cols(ssd_A_log[j].reshape(1, -1), LANES),
                     jnp.repeat(ssd_D[j], SSD_HEAD_DIM).reshape(1, d_inner),
                     ssd_gnorm[j].reshape(1, d_inner), batch, seq, t_blk, tn_in)
            h = _out_proj(y, ssd_w_out, j, h, mod, mod_row(layer, 2), seq, tm, tn_out)
        else:
            w_side = _pad_cols(fox_w_in[j][:, 3 * fox_width:], LANES)
            qkv, f_logit = _in_proj(h, g_mix, mod, mod_row(layer, 0), mod_row(layer, 1),
                                    fox_w_in, j, 3 * fox_width, w_side, seq, tm, tn_in,
                                    n_scaled=fox_width // tn_in, scale=FOX_HEAD_DIM ** -0.5 * LOG2E)
            cum, cum_t = _fcum(f_logit, _pad_cols(fox_b_f[j].reshape(1, -1), LANES), batch, seq)
            o = _attn(qkv, cum, cum_t, batch, seq, fox_heads, tq, tn_in)
            h = _out_proj(o, fox_w_out, j, h, mod, mod_row(layer, 2), seq, tm, tn_out)
        last = layer == depth - 1
        h = _mlp(h, g_mlp, mod, mod_row(layer, 3), mod_row(layer, 4), mod_row(layer, 5),
                 w_up, w_down, layer, final_norm.reshape(1, d),
                 seq, tm, tf, final_norm=last)
    return h.reshape(batch, seq, d)
```

```python
import functools

import jax
import jax.numpy as jnp
from jax import lax
from jax.experimental import pallas as pl
from jax.experimental.pallas import tpu as pltpu

F32 = jnp.float32
BF16 = jnp.bfloat16

EPS = 1e-6
N_MOD = 6
LANES = 128
SSD_HEAD_DIM = 64
SSD_STATE = 128
SSD_CONV = 4
SSD_HEADS_PER_GROUP = 8
SSD_GROUP_WIDTH = SSD_HEADS_PER_GROUP * SSD_HEAD_DIM
SSD_CHUNK = 128
FOX_HEAD_DIM = 128
LOG2E = 1.4426950408889634
VMEM_LIMIT = 56 * 1024 * 1024


def _cparams(sem):
    return pltpu.CompilerParams(dimension_semantics=sem, vmem_limit_bytes=VMEM_LIMIT)


def _silu(x):
    half = 0.5 * x
    return half + half * jnp.tanh(half)


def _softplus(x):
    return jnp.maximum(x, 0.0) + jnp.log1p(jnp.exp(-jnp.abs(x)))


def _split3(x):
    hi = x.astype(BF16)
    r1 = x - hi.astype(F32)
    mid = r1.astype(BF16)
    lo = (r1 - mid.astype(F32)).astype(BF16)
    return hi, mid, lo


def _cumsum_rows(x):
    n = x.shape[0]
    row = lax.broadcasted_iota(jnp.int32, x.shape, 0)
    k = 1
    while k < n:
        x = x + jnp.where(row >= k, pltpu.roll(x, k, 0), 0.0)
        k *= 2
    return x


def _norm_mod(x, g, shift, scale):
    ms = jnp.mean(x * x, axis=-1, keepdims=True)
    y = x * lax.rsqrt(ms + EPS) * g
    return y * (1.0 + scale) + shift


def _cast_kernel(w_ref, *o_refs, c_main):
    w = w_ref[0]
    o_refs[0][0] = w[:, :c_main].astype(BF16)
    if len(o_refs) > 1:
        side = o_refs[1]
        side[0] = jnp.zeros(side.shape[1:], BF16)
        side[0, :, 0:w.shape[1] - c_main] = w[:, c_main:].astype(BF16)


def _cast_weights(w, rows_blk, c_main=None):
    layers, r, c = w.shape
    c_main = c if c_main is None else c_main
    out_specs = [pl.BlockSpec((1, rows_blk, c_main), lambda l, i: (l, i, 0))]
    out_shape = [jax.ShapeDtypeStruct((layers, r, c_main), BF16)]
    if c_main < c:
        assert c - c_main <= LANES
        out_specs.append(pl.BlockSpec((1, rows_blk, LANES), lambda l, i: (l, i, 0)))
        out_shape.append(jax.ShapeDtypeStruct((layers, r, LANES), BF16))
    return pl.pallas_call(
        functools.partial(_cast_kernel, c_main=c_main),
        grid=(layers, r // rows_blk),
        in_specs=[pl.BlockSpec((1, rows_blk, c), lambda l, i: (l, i, 0))],
        out_specs=out_specs,
        out_shape=out_shape,
        compiler_params=_cparams(("parallel", "parallel")),
        name="cast",
    )(w)


def _ada_kernel(c_ref, w_ref, b_ref, o_ref):
    cond = _silu(c_ref[...])
    o_ref[0] = jnp.dot(cond.astype(BF16), w_ref[0].astype(BF16),
                       preferred_element_type=F32) + b_ref[0]


def _ada(c_pad, w_ada, b_ada, tn):
    depth, d, n = w_ada.shape
    rows = c_pad.shape[0]
    return pl.pallas_call(
        _ada_kernel,
        grid=(depth, n // tn),
        in_specs=[
            pl.BlockSpec((rows, d), lambda l, j: (0, 0)),
            pl.BlockSpec((1, d, tn), lambda l, j: (l, 0, j)),
            pl.BlockSpec((1, 1, tn), lambda l, j: (l, 0, j)),
        ],
        out_specs=pl.BlockSpec((1, rows, tn), lambda l, j: (l, 0, j)),
        out_shape=jax.ShapeDtypeStruct((depth, rows, n), F32),
        compiler_params=_cparams(("parallel", "parallel")),
        name="ada",
    )(c_pad, w_ada, b_ada.reshape(depth, 1, n))


def _in_proj_kernel(x_ref, g_ref, sh_ref, sc_ref, w_ref, wx_ref, o_ref, ox_ref, u_ref,
                    *, n_scaled, scale):
    j = pl.program_id(1)

    @pl.when(j == 0)
    def _():
        u = _norm_mod(x_ref[...], g_ref[...], sh_ref[0], sc_ref[0]).astype(BF16)
        u_ref[...] = u
        ox_ref[...] = jnp.dot(u, wx_ref[0], preferred_element_type=F32)

    acc = jnp.dot(u_ref[...], w_ref[0], preferred_element_type=F32)
    if n_scaled:
        acc = acc * jnp.where(j < n_scaled, scale, 1.0)
    o_ref[0] = acc.astype(o_ref.dtype)


def _in_proj(h, g, mod, sh_row, sc_row, w, w_layer, n, wx, seq, tm, tn, n_scaled=0, scale=1.0):
    m, d = h.shape
    per_b = seq // tm
    return pl.pallas_call(
        functools.partial(_in_proj_kernel, n_scaled=n_scaled, scale=scale),
        grid=(m // tm, n // tn),
        in_specs=[
            pl.BlockSpec((tm, d), lambda i, j: (i, 0)),
            pl.BlockSpec((1, d), lambda i, j: (0, 0)),
            pl.BlockSpec((1, 1, d), lambda i, j: (sh_row + i // per_b, 0, 0)),
            pl.BlockSpec((1, 1, d), lambda i, j: (sc_row + i // per_b, 0, 0)),
            pl.BlockSpec((1, d, tn), lambda i, j: (w_layer, 0, j)),
            pl.BlockSpec((1, d, LANES), lambda i, j: (w_layer, 0, 0)),
        ],
        out_specs=[
            pl.BlockSpec((1, tm, tn), lambda i, j: (j, i, 0)),
            pl.BlockSpec((tm, LANES), lambda i, j: (i, 0)),
        ],
        out_shape=[
            jax.ShapeDtypeStruct((n // tn, m, tn), BF16),
            jax.ShapeDtypeStruct((m, LANES), F32),
        ],
        scratch_shapes=[pltpu.VMEM((tm, d), BF16)],
        compiler_params=_cparams(("parallel", "arbitrary")),
        name="in_proj",
    )(h, g, mod, mod, w, wx)


def _out_proj_kernel(a_ref, w_ref, h_ref, gate_ref, o_ref):
    acc = jnp.dot(a_ref[...], w_ref[0], preferred_element_type=F32)
    o_ref[...] = h_ref[...] + gate_ref[0] * acc


def _out_proj(a, w, w_layer, h, mod, gate_row, seq, tm, tn):
    m, k = a.shape
    d = w.shape[2]
    per_b = seq // tm
    return pl.pallas_call(
        _out_proj_kernel,
        grid=(m // tm, d // tn),
        in_specs=[
            pl.BlockSpec((tm, k), lambda i, j: (i, 0)),
            pl.BlockSpec((1, k, tn), lambda i, j: (w_layer, 0, j)),
            pl.BlockSpec((tm, tn), lambda i, j: (i, j)),
            pl.BlockSpec((1, 1, tn), lambda i, j: (gate_row + i // per_b, 0, j)),
        ],
        out_specs=pl.BlockSpec((tm, tn), lambda i, j: (i, j)),
        out_shape=jax.ShapeDtypeStruct((m, d), F32),
        compiler_params=_cparams(("parallel", "arbitrary")),
        name="out_proj",
    )(a, w, h, mod)


def _mlp_kernel(h_ref, g_ref, sh_ref, sc_ref, gate_ref, wu_ref, wd_ref, fn_ref, o_ref, u_ref,
                *, final_norm):
    f = pl.program_id(1)

    @pl.when(f == 0)
    def _():
        u_ref[...] = _norm_mod(h_ref[...], g_ref[...], sh_ref[0], sc_ref[0]).astype(BF16)
        o_ref[...] = jnp.zeros_like(o_ref)

    a = jnp.dot(u_ref[...], wu_ref[0], preferred_element_type=F32)
    a = jnp.square(jnp.maximum(a, 0.0)).astype(BF16)
    o_ref[...] += jnp.dot(a, wd_ref[0], preferred_element_type=F32)

    @pl.when(f == pl.num_programs(1) - 1)
    def _():
        y = h_ref[...] + gate_ref[0] * o_ref[...]
        if final_norm:
            ms = jnp.mean(y * y, axis=-1, keepdims=True)
            y = y * lax.rsqrt(ms + EPS) * fn_ref[...]
        o_ref[...] = y


def _mlp(h, g, mod, sh_row, sc_row, gate_row, w_up, w_down, layer, fn, seq, tm, tf, final_norm):
    m, d = h.shape
    dff = w_up.shape[2]
    per_b = seq // tm
    return pl.pallas_call(
        functools.partial(_mlp_kernel, final_norm=final_norm),
        grid=(m // tm, dff // tf),
        in_specs=[
            pl.BlockSpec((tm, d), lambda i, f: (i, 0), pipeline_mode=pl.Buffered(1)),
            pl.BlockSpec((1, d), lambda i, f: (0, 0)),
            pl.BlockSpec((1, 1, d), lambda i, f: (sh_row + i // per_b, 0, 0)),
            pl.BlockSpec((1, 1, d), lambda i, f: (sc_row + i // per_b, 0, 0)),
            pl.BlockSpec((1, 1, d), lambda i, f: (gate_row + i // per_b, 0, 0)),
            pl.BlockSpec((1, d, tf), lambda i, f: (layer, 0, f)),
            pl.BlockSpec((1, tf, d), lambda i, f: (layer, f, 0)),
            pl.BlockSpec((1, d), lambda i, f: (0, 0)),
        ],
        out_specs=pl.BlockSpec((tm, d), lambda i, f: (i, 0)),
        out_shape=jax.ShapeDtypeStruct((m, d), F32),
        scratch_shapes=[pltpu.VMEM((tm, d), BF16)],
        compiler_params=_cparams(("parallel", "arbitrary")),
        name="mlp",
    )(h, g, mod, mod, mod, w_up, w_down, fn)


def _ssd_dt_kernel(dt_ref, dtb_ref, alog_ref, sc_ref, acum_ref, cd_ref):
    ck = SSD_CHUNK
    a_neg = -jnp.exp(alog_ref[...])
    for c in range(dt_ref.shape[0] // ck):
        rows = slice(c * ck, (c + 1) * ck)
        dt = _softplus(dt_ref[rows, :] + dtb_ref[...])
        acum = _cumsum_rows(dt * a_neg)
        a_last = acum[ck - 1:ck, :]
        sc_ref[0, rows, :] = dt
        sc_ref[1, rows, :] = dt * jnp.exp(a_last - acum)
        sc_ref[2, rows, :] = jnp.exp(acum)
        acum_ref[rows, :] = acum
        cd_ref[c] = jnp.broadcast_to(jnp.exp(a_last), (8, LANES))


def _ssd_dt(dt_raw, dtb, alog, blk):
    m = dt_raw.shape[0]
    per = blk // SSD_CHUNK
    return pl.pallas_call(
        _ssd_dt_kernel,
        grid=(m // blk,),
        in_specs=[
            pl.BlockSpec((blk, LANES), lambda i: (i, 0)),
            pl.BlockSpec((1, LANES), lambda i: (0, 0)),
            pl.BlockSpec((1, LANES), lambda i: (0, 0)),
        ],
        out_specs=[
            pl.BlockSpec((3, blk, LANES), lambda i: (0, i, 0)),
            pl.BlockSpec((blk, LANES), lambda i: (i, 0)),
            pl.BlockSpec((per, 8, LANES), lambda i: (i, 0, 0)),
        ],
        out_shape=[
            jax.ShapeDtypeStruct((3, m, LANES), F32),
            jax.ShapeDtypeStruct((m, LANES), F32),
            jax.ShapeDtypeStruct((m // SSD_CHUNK, 8, LANES), F32),
        ],
        compiler_params=_cparams(("parallel",)),
        name="ssd_dt",
    )(dt_raw, dtb, alog)


def _ssd_kernel(z_ref, x_ref, b_ref, c_ref, sc_ref, acum_ref, cd_ref, wx_ref, wb_ref, wc_ref,
                bx_ref, bb_ref, bc_ref, d_ref, gn_ref, o_ref, state_ref, xbc_ref, *, t_blk):
    g = pl.program_id(1)
    t = pl.program_id(2)
    ck = SSD_CHUNK
    gw = SSD_GROUP_WIDTH
    cw = gw + 2 * SSD_STATE

    @pl.when(t == 0)
    def _():
        state_ref[...] = jnp.zeros_like(state_ref)
        xbc_ref[0:ck, :] = jnp.zeros((ck, cw), BF16)

    xbc_ref[ck:ck + t_blk, 0:gw] = x_ref[0]
    xbc_ref[ck:ck + t_blk, gw:gw + SSD_STATE] = b_ref[0]
    xbc_ref[ck:ck + t_blk, gw + SSD_STATE:cw] = c_ref[0]
    conv_w = jnp.concatenate([wx_ref[...], wb_ref[...], wc_ref[...]], axis=1)
    conv_b = jnp.concatenate([bx_ref[...], bb_ref[...], bc_ref[...]], axis=1)

    sh_r = lax.broadcasted_iota(jnp.int32, ((SSD_CONV - 1) * ck, 2 * ck), 0)
    sh_c = lax.broadcasted_iota(jnp.int32, ((SSD_CONV - 1) * ck, 2 * ck), 1)
    shift = (sh_c == (sh_r & (ck - 1)) + (ck - (SSD_CONV - 1)) + (sh_r >> 7)).astype(BF16)
    assert ck == 128

    row = lax.broadcasted_iota(jnp.int32, (ck, ck), 0)
    col = lax.broadcasted_iota(jnp.int32, (ck, ck), 1)
    causal = row >= col
    low_half = col < SSD_HEAD_DIM
    e_h = lax.broadcasted_iota(jnp.int32, (LANES, gw), 0)
    e_c = lax.broadcasted_iota(jnp.int32, (LANES, gw), 1)
    expand = (e_h == g * SSD_HEADS_PER_GROUP + (e_c >> 6)).astype(BF16)
    expand2 = jnp.concatenate([expand, expand], axis=0)
    expand3 = jnp.concatenate([expand2, expand], axis=0)
    to_lane0 = (LANES - g * SSD_HEADS_PER_GROUP) % LANES

    nt = (((1,), (1,)), ((), ()))
    tn = (((0,), (0,)), ((), ()))

    def chunk(c, carry):
        r0 = pl.multiple_of(c * ck, ck)
        window = xbc_ref[pl.ds(r0, 2 * ck), :]
        back = jnp.dot(shift, window, preferred_element_type=F32)
        acc = conv_b + conv_w[SSD_CONV - 1:SSD_CONV, :] * window[ck:2 * ck].astype(F32)
        for k in range(SSD_CONV - 1):
            acc = acc + conv_w[k:k + 1, :] * back[k * ck:(k + 1) * ck]
        conv = _silu(acc)
        xs = conv[:, 0:gw]
        bm = conv[:, gw:gw + SSD_STATE].astype(BF16)
        cm = conv[:, gw + SSD_STATE:cw].astype(BF16)

        scales = jnp.concatenate([sc_ref[i, pl.ds(r0, ck), :] for i in range(3)], axis=0)
        acum = acum_ref[pl.ds(r0, ck), :]
        ex = jnp.dot(jnp.concatenate(_split3(scales)[:2], axis=1), expand2,
                     preferred_element_type=F32)
        dt_x = ex[0:ck]
        wend_x = ex[ck:2 * ck]
        ea_x = ex[2 * ck:3 * ck]
        cd_x = jnp.dot(jnp.concatenate(_split3(cd_ref[c]), axis=1), expand3,
                       preferred_element_type=F32)[0:1]
        acum_g = pltpu.roll(acum, to_lane0, 1)
        acum_t = jnp.transpose(acum_g)

        scores = lax.dot_general(cm, bm, nt, preferred_element_type=F32)
        xdt = (xs * dt_x).astype(BF16)
        y_pairs = []
        for p in range(SSD_HEADS_PER_GROUP // 2):
            x_pair = xdt[:, p * LANES:(p + 1) * LANES]
            rhs = jnp.concatenate([jnp.where(low_half, x_pair, jnp.zeros_like(x_pair)),
                                   jnp.where(low_half, jnp.zeros_like(x_pair), x_pair)], axis=0)
            lhs = []
            for j in (2 * p, 2 * p + 1):
                a_col = jnp.broadcast_to(acum_g[:, j:j + 1], (ck, ck))
                seg = a_col - acum_t[j:j + 1, :]
                decay = jnp.exp(jnp.where(causal, seg, -jnp.inf))
                lhs.append((scores * decay).astype(BF16))
            y_pairs.append(jnp.dot(jnp.concatenate(lhs, axis=1), rhs, preferred_element_type=F32))
        y = jnp.concatenate(y_pairs, axis=1)

        state = state_ref[...]
        y = y + jnp.dot(cm, state.astype(BF16), preferred_element_type=F32) * ea_x
        state_ref[...] = state * cd_x + lax.dot_general(
            bm, (xs * wend_x).astype(BF16), tn, preferred_element_type=F32)

        y = y + xs * d_ref[...]
        y = y * _silu(z_ref[0, pl.ds(r0, ck), :].astype(F32))
        ms = jnp.mean(y * y, axis=-1, keepdims=True)
        y = y * lax.rsqrt(ms + EPS) * gn_ref[...]
        o_ref[pl.ds(r0, ck), :] = y.astype(o_ref.dtype)
        return carry

    lax.fori_loop(0, t_blk // ck, chunk, 0, unroll=2)
    xbc_ref[0:ck, :] = xbc_ref[t_blk:t_blk + ck, :]


def _ssd(zxbc, scales, acum, chunk_decay, conv_w, conv_b, d_x, gnorm, batch, seq, t_blk, tn_in):
    m = acum.shape[0]
    gw = SSD_GROUP_WIDTH
    d_inner = gnorm.shape[1]
    groups = d_inner // gw
    n_t = seq // t_blk
    per_tile = tn_in // gw
    x_tile0 = d_inner // tn_in
    b_tile = 2 * d_inner // tn_in
    c_tile = b_tile + 1
    assert groups * SSD_STATE == tn_in and tn_in % gw == 0
    xw0 = 0
    bw0 = d_inner // SSD_STATE
    cw0 = bw0 + groups

    def rows(b, t):
        return b * n_t + t

    return pl.pallas_call(
        functools.partial(_ssd_kernel, t_blk=t_blk),
        grid=(batch, groups, n_t),
        in_specs=[
            pl.BlockSpec((1, t_blk, gw), lambda b, g, t: (g // per_tile, rows(b, t), g % per_tile)),
            pl.BlockSpec((1, t_blk, gw), lambda b, g, t: (x_tile0 + g // per_tile, rows(b, t), g % per_tile)),
            pl.BlockSpec((1, t_blk, SSD_STATE), lambda b, g, t: (b_tile, rows(b, t), g)),
            pl.BlockSpec((1, t_blk, SSD_STATE), lambda b, g, t: (c_tile, rows(b, t), g)),
            pl.BlockSpec((3, t_blk, LANES), lambda b, g, t: (0, rows(b, t), 0)),
            pl.BlockSpec((t_blk, LANES), lambda b, g, t: (rows(b, t), 0)),
            pl.BlockSpec((t_blk // SSD_CHUNK, 8, LANES), lambda b, g, t: (rows(b, t), 0, 0)),
            pl.BlockSpec((SSD_CONV, gw), lambda b, g, t: (0, xw0 + g)),
            pl.BlockSpec((SSD_CONV, SSD_STATE), lambda b, g, t: (0, bw0 + g)),
            pl.BlockSpec((SSD_CONV, SSD_STATE), lambda b, g, t: (0, cw0 + g)),
            pl.BlockSpec((1, gw), lambda b, g, t: (0, xw0 + g)),
            pl.BlockSpec((1, SSD_STATE), lambda b, g, t: (0, bw0 + g)),
            pl.BlockSpec((1, SSD_STATE), lambda b, g, t: (0, cw0 + g)),
            pl.BlockSpec((1, gw), lambda b, g, t: (0, g)),
            pl.BlockSpec((1, gw), lambda b, g, t: (0, g)),
        ],
        out_specs=pl.BlockSpec((t_blk, gw), lambda b, g, t: (rows(b, t), g)),
        out_shape=jax.ShapeDtypeStruct((m, d_inner), BF16),
        scratch_shapes=[
            pltpu.VMEM((SSD_STATE, gw), F32),
            pltpu.VMEM((t_blk + SSD_CHUNK, gw + 2 * SSD_STATE), BF16),
        ],
        compiler_params=_cparams(("parallel", "parallel", "arbitrary")),
        name="ssd",
    )(zxbc, zxbc, zxbc, zxbc, scales, acum, chunk_decay, conv_w, conv_w, conv_w, conv_b, conv_b, conv_b,
      d_x, gnorm)


def _fcum_kernel(f_ref, b_ref, o_ref, ot_ref):
    x = f_ref[...] + b_ref[...]
    cum = _cumsum_rows(-_softplus(-x)) * LOG2E
    o_ref[0] = cum
    ot_ref[0] = jnp.transpose(cum)


def _fcum(f_logit, b_f, batch, seq):
    return pl.pallas_call(
        _fcum_kernel,
        grid=(batch,),
        in_specs=[
            pl.BlockSpec((seq, LANES), lambda b: (b, 0)),
            pl.BlockSpec((1, LANES), lambda b: (0, 0)),
        ],
        out_specs=[
            pl.BlockSpec((1, seq, LANES), lambda b: (b, 0, 0)),
            pl.BlockSpec((1, LANES, seq), lambda b: (b, 0, 0)),
        ],
        out_shape=[
            jax.ShapeDtypeStruct((batch, seq, LANES), F32),
            jax.ShapeDtypeStruct((batch, LANES, seq), F32),
        ],
        compiler_params=_cparams(("parallel",)),
        name="fcum",
    )(f_logit, b_f)


def _key_bias_columns(cum, head):
    r = lax.broadcasted_iota(jnp.int32, (LANES, LANES), 0)
    c = lax.broadcasted_iota(jnp.int32, (LANES, LANES), 1)
    out = None
    for i, piece in enumerate(_split3(cum)):
        sel = ((r == head) & (c == i)).astype(BF16)
        term = jnp.dot(piece, sel, preferred_element_type=F32)
        out = term if out is None else out + term
    return (-out).astype(BF16)


def _attn_kernel(q_ref, k_ref, v_ref, cq_ref, ck_ref, o_ref, kx_ref, vt_ref, s_ref, p_ref, acc_ref,
                 *, tq, tk, seq):
    head = pl.program_id(1)
    qi = pl.program_id(2)
    hd = FOX_HEAD_DIM
    nt = (((1,), (1,)), ((), ()))
    assert tq == 2 * tk

    @pl.when(qi == 0)
    def _():
        for rc in range(seq // tk):
            r0 = rc * tk
            kx_ref[rc, :, 0:hd] = k_ref[0, r0:r0 + tk, :]
            kx_ref[rc, :, hd:2 * hd] = _key_bias_columns(ck_ref[0, r0:r0 + tk, :], head)
            vt_ref[rc] = jnp.transpose(v_ref[0, r0:r0 + tk, :].astype(F32)).astype(BF16)

    lane = lax.broadcasted_iota(jnp.int32, (tq, hd), 1)
    qx = jnp.concatenate([q_ref[0], (lane < 3).astype(BF16)], axis=1)
    cq = cq_ref[0, pl.ds(head % 8, 1), :]

    def s1(t, slot):
        s_ref[slot] = lax.dot_general(kx_ref[t], qx, nt, preferred_element_type=F32)

    def s2(slot, m, l, key_offset=None):
        st = s_ref[slot]
        if key_offset is not None:
            k_idx = lax.broadcasted_iota(jnp.int32, st.shape, 0) + key_offset
            q_idx = lax.broadcasted_iota(jnp.int32, st.shape, 1)
            st = jnp.where(k_idx <= q_idx, st, -jnp.inf)
        m_new = jnp.maximum(m, jnp.max(st, axis=0, keepdims=True) + cq)
        alpha = jnp.exp2(m - m_new)
        p = jnp.exp2(st - (m_new - cq))
        p_ref[slot] = p.astype(BF16)
        return m_new, alpha * l + jnp.sum(p, axis=0, keepdims=True), alpha

    def s3(t, slot, alpha):
        acc_ref[...] = alpha * acc_ref[...] + jnp.dot(vt_ref[t], p_ref[slot],
                                                      preferred_element_type=F32)

    acc_ref[...] = jnp.zeros_like(acc_ref)
    p_ref[1] = jnp.zeros(p_ref.shape[1:], BF16)
    s1(0, 0)

    def body(j, carry):
        m, l, alpha = carry
        t = 2 * j
        s3(jnp.maximum(t - 1, 0), 1, alpha)
        s1(t + 1, 1)
        m, l, alpha = s2(0, m, l)
        s3(t, 0, alpha)
        s1(t + 2, 0)
        return s2(1, m, l)

    m0 = jnp.full((1, tq), -jnp.inf, F32)
    l0 = jnp.zeros((1, tq), F32)
    m, l, alpha = lax.fori_loop(0, qi, body, (m0, l0, jnp.ones((1, tq), F32)))
    t = 2 * qi
    s3(jnp.maximum(t - 1, 0), 1, alpha)
    s1(t + 1, 1)
    m, l, alpha = s2(0, m, l, key_offset=0)
    s3(t, 0, alpha)
    m, l, alpha = s2(1, m, l, key_offset=tk)
    s3(t + 1, 1, alpha)
    o_ref[...] = jnp.transpose(acc_ref[...] / l).astype(o_ref.dtype)


def _attn(qkv, cum, cum_t, batch, seq, heads, tq, tn_in):
    m = qkv.shape[1]
    hd = FOX_HEAD_DIM
    per_tile = tn_in // hd
    tiles = heads // per_tile
    n_q = seq // tq
    tk = tq // 2
    return pl.pallas_call(
        functools.partial(_attn_kernel, tq=tq, tk=tk, seq=seq),
        grid=(batch, heads, n_q),
        in_specs=[
            pl.BlockSpec((1, tq, hd), lambda b, h, i: (h // per_tile, b * n_q + i, h % per_tile)),
            pl.BlockSpec((1, seq, hd), lambda b, h, i: (tiles + h // per_tile, b, h % per_tile)),
            pl.BlockSpec((1, seq, hd), lambda b, h, i: (2 * tiles + h // per_tile, b, h % per_tile)),
            pl.BlockSpec((1, 8, tq), lambda b, h, i: (b, h // 8, i)),
            pl.BlockSpec((1, seq, LANES), lambda b, h, i: (b, 0, 0)),
        ],
        out_specs=pl.BlockSpec((tq, hd), lambda b, h, i: (b * n_q + i, h)),
        out_shape=jax.ShapeDtypeStruct((m, heads * hd), BF16),
        scratch_shapes=[
            pltpu.VMEM((seq // tk, tk, 2 * hd), BF16),
            pltpu.VMEM((seq // tk, hd, tk), BF16),
            pltpu.VMEM((2, tk, tq), F32),
            pltpu.VMEM((2, tk, tq), BF16),
            pltpu.VMEM((hd, tq), F32),
        ],
        compiler_params=_cparams(("parallel", "parallel", "arbitrary")),
        name="attn",
    )(qkv, qkv, qkv, cum_t, cum)


def _pad_cols(a, n):
    return jnp.pad(a, ((0, 0), (0, n - a.shape[1])))


def kernel(x, c, norm_mix, norm_mlp, w_ada, b_ada, w_up, w_down, ssd_w_in, ssd_conv_w, ssd_conv_b,
           ssd_dt_bias, ssd_A_log, ssd_D, ssd_gnorm, ssd_w_out, fox_w_in, fox_b_f, fox_w_out,
           final_norm):
    batch, seq, d = x.shape
    m = batch * seq
    depth = w_ada.shape[0]
    d_inner = ssd_gnorm.shape[1]
    ssd_heads = ssd_dt_bias.shape[1]
    fox_heads = fox_b_f.shape[1]
    fox_width = fox_heads * FOX_HEAD_DIM
    assert d_inner == ssd_heads * SSD_HEAD_DIM and ssd_heads <= LANES and fox_heads <= LANES

    tm = min(1024, seq)
    tn_in = (d_inner // SSD_GROUP_WIDTH) * SSD_STATE
    tn_out = min(512, d)
    tf = min(512, w_up.shape[2])
    t_blk = min(1024, seq)
    tq = min(512, seq)
    assert fox_width % tn_in == 0 and d_inner % tn_in == 0

    rows = 8
    c_pad = jnp.pad(c, ((0, rows - batch), (0, 0)))
    mod = _ada(c_pad, w_ada, b_ada, tn=min(1024, N_MOD * d))[:, :batch]
    mod = mod.reshape(depth, batch, N_MOD, d).transpose(0, 2, 1, 3).reshape(depth * N_MOD * batch, 1, d)

    def mod_row(layer, which):
        return (layer * N_MOD + which) * batch

    w_up_b = _cast_weights(w_up, 256)[0]
    w_down_b = _cast_weights(w_down, 1024)[0]
    h = x.reshape(m, d)
    for layer in range(depth):
        j = layer // 2
        g_mix = norm_mix[layer].reshape(1, d)
        g_mlp = norm_mlp[layer].reshape(1, d)
        if layer % 2 == 0:
            n_main = 2 * d_inner + 2 * (d_inner // SSD_GROUP_WIDTH) * SSD_STATE
            w_main, w_side = _cast_weights(ssd_w_in, 256, n_main)
            zxbc, dt_raw = _in_proj(h, g_mix, mod, mod_row(layer, 0), mod_row(layer, 1),
                                    w_main, j, n_main, w_side, seq, tm, tn_in)
            scales, acum, chunk_decay = _ssd_dt(dt_raw, _pad_cols(ssd_dt_bias[j].reshape(1, -1), LANES),
                                                _pad_cols(ssd_A_log[j].reshape(1, -1), LANES), t_blk)
            y = _ssd(zxbc, scales, acum, chunk_decay, ssd_conv_w[j], ssd_conv_b[j].reshape(1, -1),
                     jnp.repeat(ssd_D[j], SSD_HEAD_DIM).reshape(1, d_inner),
                     ssd_gnorm[j].reshape(1, d_inner), batch, seq, t_blk, tn_in)
            h = _out_proj(y, _cast_weights(ssd_w_out, 1024)[0], j, h, mod, mod_row(layer, 2), seq, tm, tn_out)
        else:
            w_main, w_side = _cast_weights(fox_w_in, 256, 3 * fox_width)
            qkv, f_logit = _in_proj(h, g_mix, mod, mod_row(layer, 0), mod_row(layer, 1),
                                    w_main, j, 3 * fox_width, w_side, seq, tm, tn_in,
                                    n_scaled=fox_width // tn_in, scale=FOX_HEAD_DIM ** -0.5 * LOG2E)
            cum, cum_t = _fcum(f_logit, _pad_cols(fox_b_f[j].reshape(1, -1), LANES), batch, seq)
            o = _attn(qkv, cum, cum_t, batch, seq, fox_heads, tq, tn_in)
            h = _out_proj(o, _cast_weights(fox_w_out, 1024)[0], j, h, mod, mod_row(layer, 2), seq, tm, tn_out)
        last = layer == depth - 1
        h = _mlp(h, g_mlp, mod, mod_row(layer, 3), mod_row(layer, 4), mod_row(layer, 5),
                 w_up_b, w_down_b, layer, final_norm.reshape(1, d),
                 seq, tm, tf, final_norm=last)
    return h.reshape(batch, seq, d)
```

```python
import functools

import jax
import jax.numpy as jnp
from jax import lax
from jax.experimental import pallas as pl
from jax.experimental.pallas import tpu as pltpu

F32 = jnp.float32
BF16 = jnp.bfloat16

EPS = 1e-6
N_MOD = 6
LANES = 128
SSD_HEAD_DIM = 64
SSD_STATE = 128
SSD_CONV = 4
SSD_HEADS_PER_GROUP = 8
SSD_GROUP_WIDTH = SSD_HEADS_PER_GROUP * SSD_HEAD_DIM
SSD_CHUNK = 128
FOX_HEAD_DIM = 128
LOG2E = 1.4426950408889634
VMEM_LIMIT = 56 * 1024 * 1024


def _cparams(sem):
    return pltpu.CompilerParams(dimension_semantics=sem, vmem_limit_bytes=VMEM_LIMIT)


def _silu(x):
    half = 0.5 * x
    return half + half * jnp.tanh(half)


def _softplus(x):
    return jnp.maximum(x, 0.0) + jnp.log1p(jnp.exp(-jnp.abs(x)))


def _split3(x):
    hi = x.astype(BF16)
    r1 = x - hi.astype(F32)
    mid = r1.astype(BF16)
    lo = (r1 - mid.astype(F32)).astype(BF16)
    return hi, mid, lo


def _cumsum_rows(x):
    n = x.shape[0]
    row = lax.broadcasted_iota(jnp.int32, x.shape, 0)
    k = 1
    while k < n:
        x = x + jnp.where(row >= k, pltpu.roll(x, k, 0), 0.0)
        k *= 2
    return x


def _norm_mod(x, g, shift, scale):
    ms = jnp.mean(x * x, axis=-1, keepdims=True)
    y = x * lax.rsqrt(ms + EPS) * g
    return y * (1.0 + scale) + shift


def _cast_kernel(w_ref, o_ref):
    o_ref[...] = w_ref[...].astype(BF16)


def _cast_weights(w, rows_blk):
    layers, r, c = w.shape
    return pl.pallas_call(
        _cast_kernel,
        grid=(layers, r // rows_blk),
        in_specs=[pl.BlockSpec((1, rows_blk, c), lambda l, i: (l, i, 0))],
        out_specs=pl.BlockSpec((1, rows_blk, c), lambda l, i: (l, i, 0)),
        out_shape=jax.ShapeDtypeStruct((layers, r, c), BF16),
        compiler_params=_cparams(("parallel", "parallel")),
        name="cast",
    )(w)


def _cast_t_kernel(w_ref, o_ref):
    w = w_ref[0]
    pad = o_ref.shape[2] - w.shape[0]
    if pad:
        w = jnp.concatenate([w, jnp.zeros((pad, w.shape[1]), F32)], axis=0)
    o_ref[0] = jnp.transpose(w).astype(BF16)


def _cast_weights_t(w_t, c_main, cols_blk):
    layers, c, k = w_t.shape
    side = c - c_main
    assert 0 < side <= LANES and c_main % side == 0 and side % 8 == 0 and c_main % cols_blk == 0
    main = pl.pallas_call(
        _cast_t_kernel,
        grid=(layers, c_main // cols_blk),
        in_specs=[pl.BlockSpec((1, cols_blk, k), lambda l, i: (l, i, 0))],
        out_specs=pl.BlockSpec((1, k, cols_blk), lambda l, i: (l, 0, i)),
        out_shape=jax.ShapeDtypeStruct((layers, k, c_main), BF16),
        compiler_params=_cparams(("parallel", "parallel")),
        name="cast_t",
    )(w_t)
    tail = pl.pallas_call(
        _cast_t_kernel,
        grid=(layers,),
        in_specs=[pl.BlockSpec((1, side, k), lambda l: (l, c_main // side, 0))],
        out_specs=pl.BlockSpec((1, k, LANES), lambda l: (l, 0, 0)),
        out_shape=jax.ShapeDtypeStruct((layers, k, LANES), BF16),
        compiler_params=_cparams(("parallel",)),
        name="cast_t_side",
    )(w_t)
    return main, tail


def _ada_kernel(c_ref, w_ref, b_ref, o_ref):
    cond = _silu(c_ref[...])
    o_ref[0] = jnp.dot(cond.astype(BF16), w_ref[0].astype(BF16),
                       preferred_element_type=F32) + b_ref[0]


def _ada(c_pad, w_ada, b_ada, tn):
    depth, d, n = w_ada.shape
    rows = c_pad.shape[0]
    return pl.pallas_call(
        _ada_kernel,
        grid=(depth, n // tn),
        in_specs=[
            pl.BlockSpec((rows, d), lambda l, j: (0, 0)),
            pl.BlockSpec((1, d, tn), lambda l, j: (l, 0, j)),
            pl.BlockSpec((1, 1, tn), lambda l, j: (l, 0, j)),
        ],
        out_specs=pl.BlockSpec((1, rows, tn), lambda l, j: (l, 0, j)),
        out_shape=jax.ShapeDtypeStruct((depth, rows, n), F32),
        compiler_params=_cparams(("parallel", "parallel")),
        name="ada",
    )(c_pad, w_ada, b_ada.reshape(depth, 1, n))


def _in_proj_kernel(x_ref, g_ref, sh_ref, sc_ref, w_ref, wx_ref, o_ref, ox_ref, u_ref,
                    *, n_scaled, scale):
    j = pl.program_id(1)

    @pl.when(j == 0)
    def _():
        u = _norm_mod(x_ref[...], g_ref[...], sh_ref[0], sc_ref[0]).astype(BF16)
        u_ref[...] = u
        ox_ref[...] = jnp.dot(u, wx_ref[0], preferred_element_type=F32)

    acc = jnp.dot(u_ref[...], w_ref[0], preferred_element_type=F32)
    if n_scaled:
        acc = acc * jnp.where(j < n_scaled, scale, 1.0)
    o_ref[0] = acc.astype(o_ref.dtype)


def _in_proj(h, g, mod, sh_row, sc_row, w, w_layer, n, wx, seq, tm, tn, n_scaled=0, scale=1.0):
    m, d = h.shape
    per_b = seq // tm
    return pl.pallas_call(
        functools.partial(_in_proj_kernel, n_scaled=n_scaled, scale=scale),
        grid=(m // tm, n // tn),
        in_specs=[
            pl.BlockSpec((tm, d), lambda i, j: (i, 0)),
            pl.BlockSpec((1, d), lambda i, j: (0, 0)),
            pl.BlockSpec((1, 1, d), lambda i, j: (sh_row + i // per_b, 0, 0)),
            pl.BlockSpec((1, 1, d), lambda i, j: (sc_row + i // per_b, 0, 0)),
            pl.BlockSpec((1, d, tn), lambda i, j: (w_layer, 0, j)),
            pl.BlockSpec((1, d, LANES), lambda i, j: (w_layer, 0, 0)),
        ],
        out_specs=[
            pl.BlockSpec((1, tm, tn), lambda i, j: (j, i, 0)),
            pl.BlockSpec((tm, LANES), lambda i, j: (i, 0)),
        ],
        out_shape=[
            jax.ShapeDtypeStruct((n // tn, m, tn), BF16),
            jax.ShapeDtypeStruct((m, LANES), F32),
        ],
        scratch_shapes=[pltpu.VMEM((tm, d), BF16)],
        compiler_params=_cparams(("parallel", "arbitrary")),
        name="in_proj",
    )(h, g, mod, mod, w, wx)


def _out_proj_kernel(a_ref, w_ref, h_ref, gate_ref, o_ref):
    acc = jnp.dot(a_ref[...], w_ref[0], preferred_element_type=F32)
    o_ref[...] = h_ref[...] + gate_ref[0] * acc


def _out_proj(a, w, w_layer, h, mod, gate_row, seq, tm, tn):
    m, k = a.shape
    d = w.shape[2]
    per_b = seq // tm
    return pl.pallas_call(
        _out_proj_kernel,
        grid=(m // tm, d // tn),
        in_specs=[
            pl.BlockSpec((tm, k), lambda i, j: (i, 0)),
            pl.BlockSpec((1, k, tn), lambda i, j: (w_layer, 0, j)),
            pl.BlockSpec((tm, tn), lambda i, j: (i, j)),
            pl.BlockSpec((1, 1, tn), lambda i, j: (gate_row + i // per_b, 0, j)),
        ],
        out_specs=pl.BlockSpec((tm, tn), lambda i, j: (i, j)),
        out_shape=jax.ShapeDtypeStruct((m, d), F32),
        compiler_params=_cparams(("parallel", "arbitrary")),
        name="out_proj",
    )(a, w, h, mod)


def _mlp_kernel(h_ref, g_ref, sh_ref, sc_ref, gate_ref, wu_ref, wd_ref, fn_ref, o_ref, u_ref,
                *, final_norm):
    f = pl.program_id(1)

    @pl.when(f == 0)
    def _():
        u_ref[...] = _norm_mod(h_ref[...], g_ref[...], sh_ref[0], sc_ref[0]).astype(BF16)
        o_ref[...] = jnp.zeros_like(o_ref)

    a = jnp.dot(u_ref[...], wu_ref[0], preferred_element_type=F32)
    a = jnp.square(jnp.maximum(a, 0.0)).astype(BF16)
    o_ref[...] += jnp.dot(a, wd_ref[0], preferred_element_type=F32)

    @pl.when(f == pl.num_programs(1) - 1)
    def _():
        y = h_ref[...] + gate_ref[0] * o_ref[...]
        if final_norm:
            ms = jnp.mean(y * y, axis=-1, keepdims=True)
            y = y * lax.rsqrt(ms + EPS) * fn_ref[...]
        o_ref[...] = y


def _mlp(h, g, mod, sh_row, sc_row, gate_row, w_up, w_down, layer, fn, seq, tm, tf, final_norm):
    m, d = h.shape
    dff = w_up.shape[2]
    per_b = seq // tm
    return pl.pallas_call(
        functools.partial(_mlp_kernel, final_norm=final_norm),
        grid=(m // tm, dff // tf),
        in_specs=[
            pl.BlockSpec((tm, d), lambda i, f: (i, 0)),
            pl.BlockSpec((1, d), lambda i, f: (0, 0)),
            pl.BlockSpec((1, 1, d), lambda i, f: (sh_row + i // per_b, 0, 0)),
            pl.BlockSpec((1, 1, d), lambda i, f: (sc_row + i // per_b, 0, 0)),
            pl.BlockSpec((1, 1, d), lambda i, f: (gate_row + i // per_b, 0, 0)),
            pl.BlockSpec((1, d, tf), lambda i, f: (layer, 0, f)),
            pl.BlockSpec((1, tf, d), lambda i, f: (layer, f, 0)),
            pl.BlockSpec((1, d), lambda i, f: (0, 0)),
        ],
        out_specs=pl.BlockSpec((tm, d), lambda i, f: (i, 0)),
        out_shape=jax.ShapeDtypeStruct((m, d), F32),
        scratch_shapes=[pltpu.VMEM((tm, d), BF16)],
        compiler_params=_cparams(("parallel", "arbitrary")),
        name="mlp",
    )(h, g, mod, mod, mod, w_up, w_down, fn)


def _ssd_dt_kernel(dt_ref, dtb_ref, alog_ref, sc_ref, acum_ref, cd_ref):
    ck = SSD_CHUNK
    a_neg = -jnp.exp(alog_ref[...])
    for c in range(dt_ref.shape[0] // ck):
        rows = slice(c * ck, (c + 1) * ck)
        dt = _softplus(dt_ref[rows, :] + dtb_ref[...])
        acum = _cumsum_rows(dt * a_neg)
        a_last = acum[ck - 1:ck, :]
        sc_ref[0, rows, :] = dt
        sc_ref[1, rows, :] = dt * jnp.exp(a_last - acum)
        sc_ref[2, rows, :] = jnp.exp(acum)
        acum_ref[rows, :] = acum
        cd_ref[c] = jnp.broadcast_to(jnp.exp(a_last), (8, LANES))


def _ssd_dt(dt_raw, dtb, alog, blk):
    m = dt_raw.shape[0]
    per = blk // SSD_CHUNK
    return pl.pallas_call(
        _ssd_dt_kernel,
        grid=(m // blk,),
        in_specs=[
            pl.BlockSpec((blk, LANES), lambda i: (i, 0)),
            pl.BlockSpec((1, LANES), lambda i: (0, 0)),
            pl.BlockSpec((1, LANES), lambda i: (0, 0)),
        ],
        out_specs=[
            pl.BlockSpec((3, blk, LANES), lambda i: (0, i, 0)),
            pl.BlockSpec((blk, LANES), lambda i: (i, 0)),
            pl.BlockSpec((per, 8, LANES), lambda i: (i, 0, 0)),
        ],
        out_shape=[
            jax.ShapeDtypeStruct((3, m, LANES), F32),
            jax.ShapeDtypeStruct((m, LANES), F32),
            jax.ShapeDtypeStruct((m // SSD_CHUNK, 8, LANES), F32),
        ],
        compiler_params=_cparams(("parallel",)),
        name="ssd_dt",
    )(dt_raw, dtb, alog)


def _ssd_kernel(z_ref, x_ref, b_ref, c_ref, sc_ref, acum_ref, cd_ref, wx_ref, wb_ref, wc_ref,
                bx_ref, bb_ref, bc_ref, d_ref, gn_ref, o_ref, state_ref, xbc_ref, *, t_blk):
    g = pl.program_id(1)
    t = pl.program_id(2)
    ck = SSD_CHUNK
    gw = SSD_GROUP_WIDTH
    cw = gw + 2 * SSD_STATE

    @pl.when(t == 0)
    def _():
        state_ref[...] = jnp.zeros_like(state_ref)
        xbc_ref[0:ck, :] = jnp.zeros((ck, cw), BF16)

    xbc_ref[ck:ck + t_blk, 0:gw] = x_ref[0]
    xbc_ref[ck:ck + t_blk, gw:gw + SSD_STATE] = b_ref[0]
    xbc_ref[ck:ck + t_blk, gw + SSD_STATE:cw] = c_ref[0]
    conv_w = jnp.concatenate([wx_ref[...], wb_ref[...], wc_ref[...]], axis=1)
    conv_b = jnp.concatenate([bx_ref[...], bb_ref[...], bc_ref[...]], axis=1)

    sh_r = lax.broadcasted_iota(jnp.int32, ((SSD_CONV - 1) * ck, 2 * ck), 0)
    sh_c = lax.broadcasted_iota(jnp.int32, ((SSD_CONV - 1) * ck, 2 * ck), 1)
    shift = (sh_c == (sh_r & (ck - 1)) + (ck - (SSD_CONV - 1)) + (sh_r >> 7)).astype(BF16)
    assert ck == 128

    row = lax.broadcasted_iota(jnp.int32, (ck, ck), 0)
    col = lax.broadcasted_iota(jnp.int32, (ck, ck), 1)
    causal = row >= col
    low_half = col < SSD_HEAD_DIM
    e_h = lax.broadcasted_iota(jnp.int32, (LANES, gw), 0)
    e_c = lax.broadcasted_iota(jnp.int32, (LANES, gw), 1)
    expand = (e_h == g * SSD_HEADS_PER_GROUP + (e_c >> 6)).astype(BF16)
    expand2 = jnp.concatenate([expand, expand], axis=0)
    expand3 = jnp.concatenate([expand2, expand], axis=0)
    to_lane0 = (LANES - g * SSD_HEADS_PER_GROUP) % LANES

    nt = (((1,), (1,)), ((), ()))
    tn = (((0,), (0,)), ((), ()))

    def chunk(c, carry):
        r0 = pl.multiple_of(c * ck, ck)
        window = xbc_ref[pl.ds(r0, 2 * ck), :]
        back = jnp.dot(shift, window, preferred_element_type=F32)
        acc = conv_b + conv_w[SSD_CONV - 1:SSD_CONV, :] * window[ck:2 * ck].astype(F32)
        for k in range(SSD_CONV - 1):
            acc = acc + conv_w[k:k + 1, :] * back[k * ck:(k + 1) * ck]
        conv = _silu(acc)
        xs = conv[:, 0:gw]
        bm = conv[:, gw:gw + SSD_STATE].astype(BF16)
        cm = conv[:, gw + SSD_STATE:cw].astype(BF16)

        scales = jnp.concatenate([sc_ref[i, pl.ds(r0, ck), :] for i in range(3)], axis=0)
        acum = acum_ref[pl.ds(r0, ck), :]
        ex = jnp.dot(jnp.concatenate(_split3(scales)[:2], axis=1), expand2,
                     preferred_element_type=F32)
        dt_x = ex[0:ck]
        wend_x = ex[ck:2 * ck]
        ea_x = ex[2 * ck:3 * ck]
        cd_x = jnp.dot(jnp.concatenate(_split3(cd_ref[c]), axis=1), expand3,
                       preferred_element_type=F32)[0:1]
        acum_g = pltpu.roll(acum, to_lane0, 1)
        acum_t = jnp.transpose(acum_g)

        scores = lax.dot_general(cm, bm, nt, preferred_element_type=F32)
        xdt = (xs * dt_x).astype(BF16)
        y_pairs = []
        for p in range(SSD_HEADS_PER_GROUP // 2):
            x_pair = xdt[:, p * LANES:(p + 1) * LANES]
            rhs = jnp.concatenate([jnp.where(low_half, x_pair, jnp.zeros_like(x_pair)),
                                   jnp.where(low_half, jnp.zeros_like(x_pair), x_pair)], axis=0)
            lhs = []
            for j in (2 * p, 2 * p + 1):
                a_col = jnp.broadcast_to(acum_g[:, j:j + 1], (ck, ck))
                seg = a_col - acum_t[j:j + 1, :]
                decay = jnp.exp(jnp.where(causal, seg, -jnp.inf))
                lhs.append((scores * decay).astype(BF16))
            y_pairs.append(jnp.dot(jnp.concatenate(lhs, axis=1), rhs, preferred_element_type=F32))
        y = jnp.concatenate(y_pairs, axis=1)

        state = state_ref[...]
        y = y + jnp.dot(cm, state.astype(BF16), preferred_element_type=F32) * ea_x
        state_ref[...] = state * cd_x + lax.dot_general(
            bm, (xs * wend_x).astype(BF16), tn, preferred_element_type=F32)

        y = y + xs * d_ref[...]
        y = y * _silu(z_ref[0, pl.ds(r0, ck), :].astype(F32))
        ms = jnp.mean(y * y, axis=-1, keepdims=True)
        y = y * lax.rsqrt(ms + EPS) * gn_ref[...]
        o_ref[pl.ds(r0, ck), :] = y.astype(o_ref.dtype)
        return carry

    lax.fori_loop(0, t_blk // ck, chunk, 0, unroll=2)
    xbc_ref[0:ck, :] = xbc_ref[t_blk:t_blk + ck, :]


def _ssd(zxbc, scales, acum, chunk_decay, conv_w, conv_b, d_x, gnorm, batch, seq, t_blk, tn_in):
    m = acum.shape[0]
    gw = SSD_GROUP_WIDTH
    d_inner = gnorm.shape[1]
    groups = d_inner // gw
    n_t = seq // t_blk
    per_tile = tn_in // gw
    x_tile0 = d_inner // tn_in
    b_tile = 2 * d_inner // tn_in
    c_tile = b_tile + 1
    assert groups * SSD_STATE == tn_in and tn_in % gw == 0
    xw0 = 0
    bw0 = d_inner // SSD_STATE
    cw0 = bw0 + groups

    def rows(b, t):
        return b * n_t + t

    return pl.pallas_call(
        functools.partial(_ssd_kernel, t_blk=t_blk),
        grid=(batch, groups, n_t),
        in_specs=[
            pl.BlockSpec((1, t_blk, gw), lambda b, g, t: (g // per_tile, rows(b, t), g % per_tile)),
            pl.BlockSpec((1, t_blk, gw), lambda b, g, t: (x_tile0 + g // per_tile, rows(b, t), g % per_tile)),
            pl.BlockSpec((1, t_blk, SSD_STATE), lambda b, g, t: (b_tile, rows(b, t), g)),
            pl.BlockSpec((1, t_blk, SSD_STATE), lambda b, g, t: (c_tile, rows(b, t), g)),
            pl.BlockSpec((3, t_blk, LANES), lambda b, g, t: (0, rows(b, t), 0)),
            pl.BlockSpec((t_blk, LANES), lambda b, g, t: (rows(b, t), 0)),
            pl.BlockSpec((t_blk // SSD_CHUNK, 8, LANES), lambda b, g, t: (rows(b, t), 0, 0)),
            pl.BlockSpec((SSD_CONV, gw), lambda b, g, t: (0, xw0 + g)),
            pl.BlockSpec((SSD_CONV, SSD_STATE), lambda b, g, t: (0, bw0 + g)),
            pl.BlockSpec((SSD_CONV, SSD_STATE), lambda b, g, t: (0, cw0 + g)),
            pl.BlockSpec((1, gw), lambda b, g, t: (0, xw0 + g)),
            pl.BlockSpec((1, SSD_STATE), lambda b, g, t: (0, bw0 + g)),
            pl.BlockSpec((1, SSD_STATE), lambda b, g, t: (0, cw0 + g)),
            pl.BlockSpec((1, gw), lambda b, g, t: (0, g)),
            pl.BlockSpec((1, gw), lambda b, g, t: (0, g)),
        ],
        out_specs=pl.BlockSpec((t_blk, gw), lambda b, g, t: (rows(b, t), g)),
        out_shape=jax.ShapeDtypeStruct((m, d_inner), BF16),
        scratch_shapes=[
            pltpu.VMEM((SSD_STATE, gw), F32),
            pltpu.VMEM((t_blk + SSD_CHUNK, gw + 2 * SSD_STATE), BF16),
        ],
        compiler_params=_cparams(("parallel", "parallel", "arbitrary")),
        name="ssd",
    )(zxbc, zxbc, zxbc, zxbc, scales, acum, chunk_decay, conv_w, conv_w, conv_w, conv_b, conv_b, conv_b,
      d_x, gnorm)


def _fcum_kernel(f_ref, b_ref, o_ref, ot_ref):
    x = f_ref[...] + b_ref[...]
    cum = _cumsum_rows(-_softplus(-x)) * LOG2E
    o_ref[0] = cum
    ot_ref[0] = jnp.transpose(cum)


def _fcum(f_logit, b_f, batch, seq):
    return pl.pallas_call(
        _fcum_kernel,
        grid=(batch,),
        in_specs=[
            pl.BlockSpec((seq, LANES), lambda b: (b, 0)),
            pl.BlockSpec((1, LANES), lambda b: (0, 0)),
        ],
        out_specs=[
            pl.BlockSpec((1, seq, LANES), lambda b: (b, 0, 0)),
            pl.BlockSpec((1, LANES, seq), lambda b: (b, 0, 0)),
        ],
        out_shape=[
            jax.ShapeDtypeStruct((batch, seq, LANES), F32),
            jax.ShapeDtypeStruct((batch, LANES, seq), F32),
        ],
        compiler_params=_cparams(("parallel",)),
        name="fcum",
    )(f_logit, b_f)


def _key_bias_columns(cum, head):
    r = lax.broadcasted_iota(jnp.int32, (LANES, LANES), 0)
    c = lax.broadcasted_iota(jnp.int32, (LANES, LANES), 1)
    out = None
    for i, piece in enumerate(_split3(cum)):
        sel = ((r == head) & (c == i)).astype(BF16)
        term = jnp.dot(piece, sel, preferred_element_type=F32)
        out = term if out is None else out + term
    return (-out).astype(BF16)


def _attn_kernel(q_ref, k_ref, v_ref, cq_ref, ck_ref, o_ref, kx_ref, vt_ref, s_ref, p_ref, acc_ref,
                 *, tq, tk, seq):
    head = pl.program_id(1)
    qi = pl.program_id(2)
    hd = FOX_HEAD_DIM
    nt = (((1,), (1,)), ((), ()))
    assert tq == 2 * tk

    @pl.when(qi == 0)
    def _():
        for rc in range(seq // tk):
            r0 = rc * tk
            kx_ref[rc, :, 0:hd] = k_ref[0, r0:r0 + tk, :]
            kx_ref[rc, :, hd:2 * hd] = _key_bias_columns(ck_ref[0, r0:r0 + tk, :], head)
            vt_ref[rc] = jnp.transpose(v_ref[0, r0:r0 + tk, :].astype(F32)).astype(BF16)

    lane = lax.broadcasted_iota(jnp.int32, (tq, hd), 1)
    qx = jnp.concatenate([q_ref[0], (lane < 3).astype(BF16)], axis=1)
    cq = cq_ref[0, pl.ds(head % 8, 1), :]

    def s1(t, slot):
        s_ref[slot] = lax.dot_general(kx_ref[t], qx, nt, preferred_element_type=F32)

    def s2(slot, m, l, key_offset=None):
        st = s_ref[slot]
        if key_offset is not None:
            k_idx = lax.broadcasted_iota(jnp.int32, st.shape, 0) + key_offset
            q_idx = lax.broadcasted_iota(jnp.int32, st.shape, 1)
            st = jnp.where(k_idx <= q_idx, st, -jnp.inf)
        m_new = jnp.maximum(m, jnp.max(st, axis=0, keepdims=True) + cq)
        alpha = jnp.exp2(m - m_new)
        p = jnp.exp2(st - (m_new - cq))
        p_ref[slot] = p.astype(BF16)
        return m_new, alpha * l + jnp.sum(p, axis=0, keepdims=True), alpha

    def s3(t, slot, alpha):
        acc_ref[...] = alpha * acc_ref[...] + jnp.dot(vt_ref[t], p_ref[slot],
                                                      preferred_element_type=F32)

    acc_ref[...] = jnp.zeros_like(acc_ref)
    p_ref[1] = jnp.zeros(p_ref.shape[1:], BF16)
    s1(0, 0)

    def body(j, carry):
        m, l, alpha = carry
        t = 2 * j
        s3(jnp.maximum(t - 1, 0), 1, alpha)
        s1(t + 1, 1)
        m, l, alpha = s2(0, m, l)
        s3(t, 0, alpha)
        s1(t + 2, 0)
        return s2(1, m, l)

    m0 = jnp.full((1, tq), -jnp.inf, F32)
    l0 = jnp.zeros((1, tq), F32)
    m, l, alpha = lax.fori_loop(0, qi, body, (m0, l0, jnp.ones((1, tq), F32)))
    t = 2 * qi
    s3(jnp.maximum(t - 1, 0), 1, alpha)
    st = lax.dot_general(kx_ref[t + 1], qx[tk:tq], nt, preferred_element_type=F32)
    m, l, alpha = s2(0, m, l, key_offset=0)
    s3(t, 0, alpha)
    k_idx = lax.broadcasted_iota(jnp.int32, st.shape, 0)
    q_idx = lax.broadcasted_iota(jnp.int32, st.shape, 1)
    st = jnp.where(k_idx <= q_idx, st, -jnp.inf)
    m_hi, l_hi, cq_hi = m[:, tk:tq], l[:, tk:tq], cq[:, tk:tq]
    m_new = jnp.maximum(m_hi, jnp.max(st, axis=0, keepdims=True) + cq_hi)
    alpha_hi = jnp.exp2(m_hi - m_new)
    p = jnp.exp2(st - (m_new - cq_hi))
    l_hi = alpha_hi * l_hi + jnp.sum(p, axis=0, keepdims=True)
    acc_ref[:, tk:tq] = alpha_hi * acc_ref[:, tk:tq] + jnp.dot(
        vt_ref[t + 1], p.astype(BF16), preferred_element_type=F32)
    l = jnp.concatenate([l[:, 0:tk], l_hi], axis=1)
    o_ref[...] = jnp.transpose(acc_ref[...] / l).astype(o_ref.dtype)


def _attn(qkv, cum, cum_t, batch, seq, heads, tq, tn_in):
    m = qkv.shape[1]
    hd = FOX_HEAD_DIM
    per_tile = tn_in // hd
    tiles = heads // per_tile
    n_q = seq // tq
    tk = tq // 2
    return pl.pallas_call(
        functools.partial(_attn_kernel, tq=tq, tk=tk, seq=seq),
        grid=(batch, heads, n_q),
        in_specs=[
            pl.BlockSpec((1, tq, hd), lambda b, h, i: (h // per_tile, b * n_q + i, h % per_tile)),
            pl.BlockSpec((1, seq, hd), lambda b, h, i: (tiles + h // per_tile, b, h % per_tile)),
            pl.BlockSpec((1, seq, hd), lambda b, h, i: (2 * tiles + h // per_tile, b, h % per_tile)),
            pl.BlockSpec((1, 8, tq), lambda b, h, i: (b, h // 8, i)),
            pl.BlockSpec((1, seq, LANES), lambda b, h, i: (b, 0, 0)),
        ],
        out_specs=pl.BlockSpec((tq, hd), lambda b, h, i: (b * n_q + i, h)),
        out_shape=jax.ShapeDtypeStruct((m, heads * hd), BF16),
        scratch_shapes=[
            pltpu.VMEM((seq // tk, tk, 2 * hd), BF16),
            pltpu.VMEM((seq // tk, hd, tk), BF16),
            pltpu.VMEM((2, tk, tq), F32),
            pltpu.VMEM((2, tk, tq), BF16),
            pltpu.VMEM((hd, tq), F32),
        ],
        compiler_params=_cparams(("parallel", "parallel", "arbitrary")),
        name="attn",
    )(qkv, qkv, qkv, cum_t, cum)


def _pad_cols(a, n):
    return jnp.pad(a, ((0, 0), (0, n - a.shape[1])))


def kernel(x, c, norm_mix, norm_mlp, w_ada, b_ada, w_up, w_down, ssd_w_in, ssd_conv_w, ssd_conv_b,
           ssd_dt_bias, ssd_A_log, ssd_D, ssd_gnorm, ssd_w_out, fox_w_in, fox_b_f, fox_w_out,
           final_norm):
    batch, seq, d = x.shape
    m = batch * seq
    depth = w_ada.shape[0]
    d_inner = ssd_gnorm.shape[1]
    ssd_heads = ssd_dt_bias.shape[1]
    fox_heads = fox_b_f.shape[1]
    fox_width = fox_heads * FOX_HEAD_DIM
    assert d_inner == ssd_heads * SSD_HEAD_DIM and ssd_heads <= LANES and fox_heads <= LANES

    tm = min(1024, seq)
    tn_in = (d_inner // SSD_GROUP_WIDTH) * SSD_STATE
    tn_out = min(512, d)
    tf = min(512, w_up.shape[2])
    t_blk = min(1024, seq)
    tq = min(512, seq)
    assert fox_width % tn_in == 0 and d_inner % tn_in == 0

    rows = 8
    c_pad = jnp.pad(c, ((0, rows - batch), (0, 0)))
    mod = _ada(c_pad, w_ada, b_ada, tn=min(1024, N_MOD * d))[:, :batch]
    mod = mod.reshape(depth, batch, N_MOD, d).transpose(0, 2, 1, 3).reshape(depth * N_MOD * batch, 1, d)

    def mod_row(layer, which):
        return (layer * N_MOD + which) * batch

    w_up_b = _cast_weights(w_up, 256)
    w_down_b = _cast_weights(w_down, 1024)
    h = x.reshape(m, d)
    for layer in range(depth):
        j = layer // 2
        g_mix = norm_mix[layer].reshape(1, d)
        g_mlp = norm_mlp[layer].reshape(1, d)
        if layer % 2 == 0:
            n_main = 2 * d_inner + 2 * (d_inner // SSD_GROUP_WIDTH) * SSD_STATE
            w_main, w_side = _cast_weights_t(jnp.transpose(ssd_w_in, (0, 2, 1)), n_main, tn_in)
            zxbc, dt_raw = _in_proj(h, g_mix, mod, mod_row(layer, 0), mod_row(layer, 1),
                                    w_main, j, n_main, w_side, seq, tm, tn_in)
            scales, acum, chunk_decay = _ssd_dt(dt_raw, _pad_cols(ssd_dt_bias[j].reshape(1, -1), LANES),
                                                _pad_cols(ssd_A_log[j].reshape(1, -1), LANES), t_blk)
            y = _ssd(zxbc, scales, acum, chunk_decay, ssd_conv_w[j], ssd_conv_b[j].reshape(1, -1),
                     jnp.repeat(ssd_D[j], SSD_HEAD_DIM).reshape(1, d_inner),
                     ssd_gnorm[j].reshape(1, d_inner), batch, seq, t_blk, tn_in)
            h = _out_proj(y, _cast_weights(ssd_w_out, 1024), j, h, mod, mod_row(layer, 2), seq, tm, tn_out)
        else:
            w_main, w_side = _cast_weights_t(jnp.transpose(fox_w_in, (0, 2, 1)), 3 * fox_width, tn_in)
            qkv, f_logit = _in_proj(h, g_mix, mod, mod_row(layer, 0), mod_row(layer, 1),
                                    w_main, j, 3 * fox_width, w_side, seq, tm, tn_in,
                                    n_scaled=fox_width // tn_in, scale=FOX_HEAD_DIM ** -0.5 * LOG2E)
            cum, cum_t = _fcum(f_logit, _pad_cols(fox_b_f[j].reshape(1, -1), LANES), batch, seq)
            o = _attn(qkv, cum, cum_t, batch, seq, fox_heads, tq, tn_in)
            h = _out_proj(o, _cast_weights(fox_w_out, 1024), j, h, mod, mod_row(layer, 2), seq, tm, tn_out)
        last = layer == depth - 1
        h = _mlp(h, g_mlp, mod, mod_row(layer, 3), mod_row(layer, 4), mod_row(layer, 5),
                 w_up_b, w_down_b, layer, final_norm.reshape(1, d),
                 seq, tm, tf, final_norm=last)
    return h.reshape(batch, seq, d)
```

```python
import functools

import jax
import jax.numpy as jnp
from jax import lax
from jax.experimental import pallas as pl
from jax.experimental.pallas import tpu as pltpu

F32 = jnp.float32
BF16 = jnp.bfloat16

EPS = 1e-6
N_MOD = 6
LANES = 128
SSD_HEAD_DIM = 64
SSD_STATE = 128
SSD_CONV = 4
SSD_HEADS_PER_GROUP = 8
SSD_GROUP_WIDTH = SSD_HEADS_PER_GROUP * SSD_HEAD_DIM
SSD_CHUNK = 128
FOX_HEAD_DIM = 128
LOG2E = 1.4426950408889634
VMEM_LIMIT = 56 * 1024 * 1024


def _cparams(sem):
    return pltpu.CompilerParams(dimension_semantics=sem, vmem_limit_bytes=VMEM_LIMIT)


def _silu(x):
    half = 0.5 * x
    return half + half * jnp.tanh(half)


def _softplus(x):
    return jnp.maximum(x, 0.0) + jnp.log1p(jnp.exp(-jnp.abs(x)))


def _split3(x):
    hi = x.astype(BF16)
    r1 = x - hi.astype(F32)
    mid = r1.astype(BF16)
    lo = (r1 - mid.astype(F32)).astype(BF16)
    return hi, mid, lo


def _cumsum_rows(x):
    n = x.shape[0]
    row = lax.broadcasted_iota(jnp.int32, x.shape, 0)
    k = 1
    while k < n:
        x = x + jnp.where(row >= k, pltpu.roll(x, k, 0), 0.0)
        k *= 2
    return x


def _norm_mod(x, g, shift, scale):
    ms = jnp.mean(x * x, axis=-1, keepdims=True)
    y = x * lax.rsqrt(ms + EPS) * g
    return y * (1.0 + scale) + shift


def _cast_t_kernel(w_ref, o_ref):
    w = w_ref[0]
    pad = o_ref.shape[2] - w.shape[0]
    if pad:
        w = jnp.concatenate([w, jnp.zeros((pad, w.shape[1]), F32)], axis=0)
    o_ref[0] = jnp.transpose(w).astype(BF16)


def _cast_weights_t(w_t, c_main, cols_blk):
    layers, c, k = w_t.shape
    side = c - c_main
    assert 0 < side <= LANES and c_main % side == 0 and side % 8 == 0 and c_main % cols_blk == 0
    main = pl.pallas_call(
        _cast_t_kernel,
        grid=(layers, c_main // cols_blk),
        in_specs=[pl.BlockSpec((1, cols_blk, k), lambda l, i: (l, i, 0))],
        out_specs=pl.BlockSpec((1, k, cols_blk), lambda l, i: (l, 0, i)),
        out_shape=jax.ShapeDtypeStruct((layers, k, c_main), BF16),
        compiler_params=_cparams(("parallel", "parallel")),
        name="cast_t",
    )(w_t)
    tail = pl.pallas_call(
        _cast_t_kernel,
        grid=(layers,),
        in_specs=[pl.BlockSpec((1, side, k), lambda l: (l, c_main // side, 0))],
        out_specs=pl.BlockSpec((1, k, LANES), lambda l: (l, 0, 0)),
        out_shape=jax.ShapeDtypeStruct((layers, k, LANES), BF16),
        compiler_params=_cparams(("parallel",)),
        name="cast_t_side",
    )(w_t)
    return main, tail


def _ada_kernel(c_ref, w_ref, b_ref, o_ref):
    cond = _silu(c_ref[...])
    o_ref[0] = jnp.dot(cond.astype(BF16), w_ref[0].astype(BF16),
                       preferred_element_type=F32) + b_ref[0]


def _ada(c_pad, w_ada, b_ada, tn):
    depth, d, n = w_ada.shape
    rows = c_pad.shape[0]
    return pl.pallas_call(
        _ada_kernel,
        grid=(depth, n // tn),
        in_specs=[
            pl.BlockSpec((rows, d), lambda l, j: (0, 0)),
            pl.BlockSpec((1, d, tn), lambda l, j: (l, 0, j)),
            pl.BlockSpec((1, 1, tn), lambda l, j: (l, 0, j)),
        ],
        out_specs=pl.BlockSpec((1, rows, tn), lambda l, j: (l, 0, j)),
        out_shape=jax.ShapeDtypeStruct((depth, rows, n), F32),
        compiler_params=_cparams(("parallel", "parallel")),
        name="ada",
    )(c_pad, w_ada, b_ada.reshape(depth, 1, n))


def _in_proj_kernel(x_ref, g_ref, sh_ref, sc_ref, w_ref, wx_ref, *rest, n_scaled, scale, n_ride):
    ride_in = rest[:n_ride]
    o_ref, ox_ref = rest[n_ride:n_ride + 2]
    ride_out = rest[n_ride + 2:2 * n_ride + 2]
    u_ref = rest[-1]
    j = pl.program_id(1)

    @pl.when(j == 0)
    def _():
        u = _norm_mod(x_ref[...], g_ref[...], sh_ref[0], sc_ref[0]).astype(BF16)
        u_ref[...] = u
        ox_ref[...] = jnp.dot(u, wx_ref[0], preferred_element_type=F32)

    acc = jnp.dot(u_ref[...], w_ref[0], preferred_element_type=F32)
    if n_scaled:
        acc = acc * jnp.where(j < n_scaled, scale, 1.0)
    o_ref[0] = acc.astype(o_ref.dtype)
    for src, dst in zip(ride_in, ride_out):
        dst[...] = src[...].astype(BF16)


def _ride_rows(w, steps):
    layers, r, _ = w.shape
    for rb in range(16, r + 1, 16):
        if r % rb == 0 and layers * (r // rb) <= steps:
            return rb
    raise ValueError("weight too large to ride along")


def _in_proj(h, g, mod, sh_row, sc_row, w, w_layer, n, wx, seq, tm, tn, n_scaled=0, scale=1.0,
             ride_along=()):
    m, d = h.shape
    per_b = seq // tm
    n_j = n // tn
    steps = (m // tm) * n_j
    ride_specs, ride_shapes = [], []
    for wr in ride_along:
        rb = _ride_rows(wr, steps)
        per_layer = wr.shape[1] // rb
        last = wr.shape[0] * per_layer - 1

        def slab(i, j, per_layer=per_layer, last=last):
            s = jnp.minimum(i * n_j + j, last)
            return (s // per_layer, s % per_layer, 0)

        ride_specs.append(pl.BlockSpec((1, rb, wr.shape[2]), slab))
        ride_shapes.append(jax.ShapeDtypeStruct(wr.shape, BF16))
    return pl.pallas_call(
        functools.partial(_in_proj_kernel, n_scaled=n_scaled, scale=scale, n_ride=len(ride_along)),
        grid=(m // tm, n_j),
        in_specs=[
            pl.BlockSpec((tm, d), lambda i, j: (i, 0)),
            pl.BlockSpec((1, d), lambda i, j: (0, 0)),
            pl.BlockSpec((1, 1, d), lambda i, j: (sh_row + i // per_b, 0, 0)),
            pl.BlockSpec((1, 1, d), lambda i, j: (sc_row + i // per_b, 0, 0)),
            pl.BlockSpec((1, d, tn), lambda i, j: (w_layer, 0, j)),
            pl.BlockSpec((1, d, LANES), lambda i, j: (w_layer, 0, 0)),
        ] + ride_specs,
        out_specs=[
            pl.BlockSpec((1, tm, tn), lambda i, j: (j, i, 0)),
            pl.BlockSpec((tm, LANES), lambda i, j: (i, 0)),
        ] + ride_specs,
        out_shape=[
            jax.ShapeDtypeStruct((n_j, m, tn), BF16),
            jax.ShapeDtypeStruct((m, LANES), F32),
        ] + ride_shapes,
        scratch_shapes=[pltpu.VMEM((tm, d), BF16)],
        compiler_params=_cparams(("arbitrary", "arbitrary")),
        name="in_proj",
    )(h, g, mod, mod, w, wx, *ride_along)


def _out_proj_kernel(a_ref, w_ref, h_ref, gate_ref, o_ref):
    acc = jnp.dot(a_ref[...], w_ref[0], preferred_element_type=F32)
    o_ref[...] = h_ref[...] + gate_ref[0] * acc


def _out_proj(a, w, w_layer, h, mod, gate_row, seq, tm, tn):
    m, k = a.shape
    d = w.shape[2]
    per_b = seq // tm
    return pl.pallas_call(
        _out_proj_kernel,
        grid=(m // tm, d // tn),
        in_specs=[
            pl.BlockSpec((tm, k), lambda i, j: (i, 0)),
            pl.BlockSpec((1, k, tn), lambda i, j: (w_layer, 0, j)),
            pl.BlockSpec((tm, tn), lambda i, j: (i, j)),
            pl.BlockSpec((1, 1, tn), lambda i, j: (gate_row + i // per_b, 0, j)),
        ],
        out_specs=pl.BlockSpec((tm, tn), lambda i, j: (i, j)),
        out_shape=jax.ShapeDtypeStruct((m, d), F32),
        compiler_params=_cparams(("parallel", "arbitrary")),
        name="out_proj",
    )(a, w, h, mod)


def _mlp_kernel(h_ref, g_ref, sh_ref, sc_ref, gate_ref, wu_ref, wd_ref, fn_ref, o_ref, u_ref,
                *, final_norm):
    f = pl.program_id(1)

    @pl.when(f == 0)
    def _():
        u_ref[...] = _norm_mod(h_ref[...], g_ref[...], sh_ref[0], sc_ref[0]).astype(BF16)
        o_ref[...] = jnp.zeros_like(o_ref)

    a = jnp.dot(u_ref[...], wu_ref[0], preferred_element_type=F32)
    a = jnp.square(jnp.maximum(a, 0.0)).astype(BF16)
    o_ref[...] += jnp.dot(a, wd_ref[0], preferred_element_type=F32)

    @pl.when(f == pl.num_programs(1) - 1)
    def _():
        y = h_ref[...] + gate_ref[0] * o_ref[...]
        if final_norm:
            ms = jnp.mean(y * y, axis=-1, keepdims=True)
            y = y * lax.rsqrt(ms + EPS) * fn_ref[...]
        o_ref[...] = y


def _mlp(h, g, mod, sh_row, sc_row, gate_row, w_up, w_down, layer, fn, seq, tm, tf, final_norm):
    m, d = h.shape
    dff = w_up.shape[2]
    per_b = seq // tm
    return pl.pallas_call(
        functools.partial(_mlp_kernel, final_norm=final_norm),
        grid=(m // tm, dff // tf),
        in_specs=[
            pl.BlockSpec((tm, d), lambda i, f: (i, 0)),
            pl.BlockSpec((1, d), lambda i, f: (0, 0)),
            pl.BlockSpec((1, 1, d), lambda i, f: (sh_row + i // per_b, 0, 0)),
            pl.BlockSpec((1, 1, d), lambda i, f: (sc_row + i // per_b, 0, 0)),
            pl.BlockSpec((1, 1, d), lambda i, f: (gate_row + i // per_b, 0, 0)),
            pl.BlockSpec((1, d, tf), lambda i, f: (layer, 0, f)),
            pl.BlockSpec((1, tf, d), lambda i, f: (layer, f, 0)),
            pl.BlockSpec((1, d), lambda i, f: (0, 0)),
        ],
        out_specs=pl.BlockSpec((tm, d), lambda i, f: (i, 0)),
        out_shape=jax.ShapeDtypeStruct((m, d), F32),
        scratch_shapes=[pltpu.VMEM((tm, d), BF16)],
        compiler_params=_cparams(("parallel", "arbitrary")),
        name="mlp",
    )(h, g, mod, mod, mod, w_up, w_down, fn)


def _ssd_dt_kernel(dt_ref, dtb_ref, alog_ref, sc_ref, acum_ref, cd_ref):
    ck = SSD_CHUNK
    a_neg = -jnp.exp(alog_ref[...])
    for c in range(dt_ref.shape[0] // ck):
        rows = slice(c * ck, (c + 1) * ck)
        dt = _softplus(dt_ref[rows, :] + dtb_ref[...])
        acum = _cumsum_rows(dt * a_neg)
        a_last = acum[ck - 1:ck, :]
        sc_ref[0, rows, :] = dt
        sc_ref[1, rows, :] = dt * jnp.exp(a_last - acum)
        sc_ref[2, rows, :] = jnp.exp(acum)
        acum_ref[rows, :] = acum
        cd_ref[c] = jnp.broadcast_to(jnp.exp(a_last), (8, LANES))


def _ssd_dt(dt_raw, dtb, alog, blk):
    m = dt_raw.shape[0]
    per = blk // SSD_CHUNK
    return pl.pallas_call(
        _ssd_dt_kernel,
        grid=(m // blk,),
        in_specs=[
            pl.BlockSpec((blk, LANES), lambda i: (i, 0)),
            pl.BlockSpec((1, LANES), lambda i: (0, 0)),
            pl.BlockSpec((1, LANES), lambda i: (0, 0)),
        ],
        out_specs=[
            pl.BlockSpec((3, blk, LANES), lambda i: (0, i, 0)),
            pl.BlockSpec((blk, LANES), lambda i: (i, 0)),
            pl.BlockSpec((per, 8, LANES), lambda i: (i, 0, 0)),
        ],
        out_shape=[
            jax.ShapeDtypeStruct((3, m, LANES), F32),
            jax.ShapeDtypeStruct((m, LANES), F32),
            jax.ShapeDtypeStruct((m // SSD_CHUNK, 8, LANES), F32),
        ],
        compiler_params=_cparams(("parallel",)),
        name="ssd_dt",
    )(dt_raw, dtb, alog)


def _ssd_kernel(z_ref, x_ref, b_ref, c_ref, sc_ref, acum_ref, cd_ref, wx_ref, wb_ref, wc_ref,
                bx_ref, bb_ref, bc_ref, d_ref, gn_ref, o_ref, state_ref, xbc_ref, *, t_blk):
    g = pl.program_id(1)
    t = pl.program_id(2)
    ck = SSD_CHUNK
    gw = SSD_GROUP_WIDTH
    cw = gw + 2 * SSD_STATE

    @pl.when(t == 0)
    def _():
        state_ref[...] = jnp.zeros_like(state_ref)
        xbc_ref[0:ck, :] = jnp.zeros((ck, cw), BF16)

    xbc_ref[ck:ck + t_blk, 0:gw] = x_ref[0]
    xbc_ref[ck:ck + t_blk, gw:gw + SSD_STATE] = b_ref[0]
    xbc_ref[ck:ck + t_blk, gw + SSD_STATE:cw] = c_ref[0]
    conv_w = jnp.concatenate([wx_ref[...], wb_ref[...], wc_ref[...]], axis=1)
    conv_b = jnp.concatenate([bx_ref[...], bb_ref[...], bc_ref[...]], axis=1)

    sh_r = lax.broadcasted_iota(jnp.int32, ((SSD_CONV - 1) * ck, 2 * ck), 0)
    sh_c = lax.broadcasted_iota(jnp.int32, ((SSD_CONV - 1) * ck, 2 * ck), 1)
    shift = (sh_c == (sh_r & (ck - 1)) + (ck - (SSD_CONV - 1)) + (sh_r >> 7)).astype(BF16)
    assert ck == 128

    row = lax.broadcasted_iota(jnp.int32, (ck, ck), 0)
    col = lax.broadcasted_iota(jnp.int32, (ck, ck), 1)
    causal = row >= col
    low_half = col < SSD_HEAD_DIM
    e_h = lax.broadcasted_iota(jnp.int32, (LANES, gw), 0)
    e_c = lax.broadcasted_iota(jnp.int32, (LANES, gw), 1)
    expand = (e_h == g * SSD_HEADS_PER_GROUP + (e_c >> 6)).astype(BF16)
    expand2 = jnp.concatenate([expand, expand], axis=0)
    expand3 = jnp.concatenate([expand2, expand], axis=0)
    to_lane0 = (LANES - g * SSD_HEADS_PER_GROUP) % LANES

    nt = (((1,), (1,)), ((), ()))
    tn = (((0,), (0,)), ((), ()))

    def chunk(c, carry):
        r0 = pl.multiple_of(c * ck, ck)
        window = xbc_ref[pl.ds(r0, 2 * ck), :]
        back = jnp.dot(shift, window, preferred_element_type=F32)
        acc = conv_b + conv_w[SSD_CONV - 1:SSD_CONV, :] * window[ck:2 * ck].astype(F32)
        for k in range(SSD_CONV - 1):
            acc = acc + conv_w[k:k + 1, :] * back[k * ck:(k + 1) * ck]
        conv = _silu(acc)
        xs = conv[:, 0:gw]
        bm = conv[:, gw:gw + SSD_STATE].astype(BF16)
        cm = conv[:, gw + SSD_STATE:cw].astype(BF16)

        scales = jnp.concatenate([sc_ref[i, pl.ds(r0, ck), :] for i in range(3)], axis=0)
        acum = acum_ref[pl.ds(r0, ck), :]
        ex = jnp.dot(jnp.concatenate(_split3(scales)[:2], axis=1), expand2,
                     preferred_element_type=F32)
        dt_x = ex[0:ck]
        wend_x = ex[ck:2 * ck]
        ea_x = ex[2 * ck:3 * ck]
        cd_x = jnp.dot(jnp.concatenate(_split3(cd_ref[c]), axis=1), expand3,
                       preferred_element_type=F32)[0:1]
        acum_g = pltpu.roll(acum, to_lane0, 1)
        acum_t = jnp.transpose(acum_g)

        scores = lax.dot_general(cm, bm, nt, preferred_element_type=F32)
        xdt = (xs * dt_x).astype(BF16)
        y_pairs = []
        for p in range(SSD_HEADS_PER_GROUP // 2):
            x_pair = xdt[:, p * LANES:(p + 1) * LANES]
            rhs = jnp.concatenate([jnp.where(low_half, x_pair, jnp.zeros_like(x_pair)),
                                   jnp.where(low_half, jnp.zeros_like(x_pair), x_pair)], axis=0)
            lhs = []
            for j in (2 * p, 2 * p + 1):
                a_col = jnp.broadcast_to(acum_g[:, j:j + 1], (ck, ck))
                seg = a_col - acum_t[j:j + 1, :]
                decay = jnp.exp(jnp.where(causal, seg, -jnp.inf))
                lhs.append((scores * decay).astype(BF16))
            y_pairs.append(jnp.dot(jnp.concatenate(lhs, axis=1), rhs, preferred_element_type=F32))
        y = jnp.concatenate(y_pairs, axis=1)

        state = state_ref[...]
        y = y + jnp.dot(cm, state.astype(BF16), preferred_element_type=F32) * ea_x
        state_ref[...] = state * cd_x + lax.dot_general(
            bm, (xs * wend_x).astype(BF16), tn, preferred_element_type=F32)

        y = y + xs * d_ref[...]
        y = y * _silu(z_ref[0, pl.ds(r0, ck), :].astype(F32))
        ms = jnp.mean(y * y, axis=-1, keepdims=True)
        y = y * lax.rsqrt(ms + EPS) * gn_ref[...]
        o_ref[pl.ds(r0, ck), :] = y.astype(o_ref.dtype)
        return carry

    lax.fori_loop(0, t_blk // ck, chunk, 0, unroll=2)
    xbc_ref[0:ck, :] = xbc_ref[t_blk:t_blk + ck, :]


def _ssd(zxbc, scales, acum, chunk_decay, conv_w, conv_b, d_x, gnorm, batch, seq, t_blk, tn_in):
    m = acum.shape[0]
    gw = SSD_GROUP_WIDTH
    d_inner = gnorm.shape[1]
    groups = d_inner // gw
    n_t = seq // t_blk
    per_tile = tn_in // gw
    x_tile0 = d_inner // tn_in
    b_tile = 2 * d_inner // tn_in
    c_tile = b_tile + 1
    assert groups * SSD_STATE == tn_in and tn_in % gw == 0
    xw0 = 0
    bw0 = d_inner // SSD_STATE
    cw0 = bw0 + groups

    def rows(b, t):
        return b * n_t + t

    return pl.pallas_call(
        functools.partial(_ssd_kernel, t_blk=t_blk),
        grid=(batch, groups, n_t),
        in_specs=[
            pl.BlockSpec((1, t_blk, gw), lambda b, g, t: (g // per_tile, rows(b, t), g % per_tile)),
            pl.BlockSpec((1, t_blk, gw), lambda b, g, t: (x_tile0 + g // per_tile, rows(b, t), g % per_tile)),
            pl.BlockSpec((1, t_blk, SSD_STATE), lambda b, g, t: (b_tile, rows(b, t), g)),
            pl.BlockSpec((1, t_blk, SSD_STATE), lambda b, g, t: (c_tile, rows(b, t), g)),
            pl.BlockSpec((3, t_blk, LANES), lambda b, g, t: (0, rows(b, t), 0)),
            pl.BlockSpec((t_blk, LANES), lambda b, g, t: (rows(b, t), 0)),
            pl.BlockSpec((t_blk // SSD_CHUNK, 8, LANES), lambda b, g, t: (rows(b, t), 0, 0)),
            pl.BlockSpec((SSD_CONV, gw), lambda b, g, t: (0, xw0 + g)),
            pl.BlockSpec((SSD_CONV, SSD_STATE), lambda b, g, t: (0, bw0 + g)),
            pl.BlockSpec((SSD_CONV, SSD_STATE), lambda b, g, t: (0, cw0 + g)),
            pl.BlockSpec((1, gw), lambda b, g, t: (0, xw0 + g)),
            pl.BlockSpec((1, SSD_STATE), lambda b, g, t: (0, bw0 + g)),
            pl.BlockSpec((1, SSD_STATE), lambda b, g, t: (0, cw0 + g)),
            pl.BlockSpec((1, gw), lambda b, g, t: (0, g)),
            pl.BlockSpec((1, gw), lambda b, g, t: (0, g)),
        ],
        out_specs=pl.BlockSpec((t_blk, gw), lambda b, g, t: (rows(b, t), g)),
        out_shape=jax.ShapeDtypeStruct((m, d_inner), BF16),
        scratch_shapes=[
            pltpu.VMEM((SSD_STATE, gw), F32),
            pltpu.VMEM((t_blk + SSD_CHUNK, gw + 2 * SSD_STATE), BF16),
        ],
        compiler_params=_cparams(("parallel", "parallel", "arbitrary")),
        name="ssd",
    )(zxbc, zxbc, zxbc, zxbc, scales, acum, chunk_decay, conv_w, conv_w, conv_w, conv_b, conv_b, conv_b,
      d_x, gnorm)


def _fcum_kernel(f_ref, b_ref, o_ref, ot_ref):
    x = f_ref[...] + b_ref[...]
    cum = _cumsum_rows(-_softplus(-x)) * LOG2E
    o_ref[0] = cum
    ot_ref[0] = jnp.transpose(cum)


def _fcum(f_logit, b_f, batch, seq):
    return pl.pallas_call(
        _fcum_kernel,
        grid=(batch,),
        in_specs=[
            pl.BlockSpec((seq, LANES), lambda b: (b, 0)),
            pl.BlockSpec((1, LANES), lambda b: (0, 0)),
        ],
        out_specs=[
            pl.BlockSpec((1, seq, LANES), lambda b: (b, 0, 0)),
            pl.BlockSpec((1, LANES, seq), lambda b: (b, 0, 0)),
        ],
        out_shape=[
            jax.ShapeDtypeStruct((batch, seq, LANES), F32),
            jax.ShapeDtypeStruct((batch, LANES, seq), F32),
        ],
        compiler_params=_cparams(("parallel",)),
        name="fcum",
    )(f_logit, b_f)


def _key_bias_columns(cum, head):
    r = lax.broadcasted_iota(jnp.int32, (LANES, LANES), 0)
    c = lax.broadcasted_iota(jnp.int32, (LANES, LANES), 1)
    out = None
    for i, piece in enumerate(_split3(cum)):
        sel = ((r == head) & (c == i)).astype(BF16)
        term = jnp.dot(piece, sel, preferred_element_type=F32)
        out = term if out is None else out + term
    return (-out).astype(BF16)


def _attn_kernel(q_ref, k_ref, v_ref, cq_ref, ck_ref, o_ref, kx_ref, vt_ref, s_ref, p_ref, acc_ref,
                 *, tq, tk, seq):
    head = pl.program_id(1)
    hd = FOX_HEAD_DIM
    nt = (((1,), (1,)), ((), ()))
    assert tq == 2 * tk

    for rc in range(seq // tk):
        r0 = rc * tk
        kx_ref[rc, :, 0:hd] = k_ref[0, r0:r0 + tk, :]
        kx_ref[rc, :, hd:2 * hd] = _key_bias_columns(ck_ref[0, r0:r0 + tk, :], head)
        vt_ref[rc] = jnp.transpose(v_ref[0, r0:r0 + tk, :].astype(F32)).astype(BF16)

    def qblock(qi, carry_q):
        lane = lax.broadcasted_iota(jnp.int32, (tq, hd), 1)
        q0 = pl.multiple_of(qi * tq, tq)
        qx = jnp.concatenate([q_ref[0, pl.ds(q0, tq), :], (lane < 3).astype(BF16)], axis=1)
        cq = cq_ref[0, head % 8, pl.ds(qi, 1), :]

        def s1(t, slot):
            s_ref[slot] = lax.dot_general(kx_ref[t], qx, nt, preferred_element_type=F32)

        def s2(slot, m, l, key_offset=None):
            st = s_ref[slot]
            if key_offset is not None:
                k_idx = lax.broadcasted_iota(jnp.int32, st.shape, 0) + key_offset
                q_idx = lax.broadcasted_iota(jnp.int32, st.shape, 1)
                st = jnp.where(k_idx <= q_idx, st, -jnp.inf)
            m_new = jnp.maximum(m, jnp.max(st, axis=0, keepdims=True) + cq)
            alpha = jnp.exp2(m - m_new)
            p = jnp.exp2(st - (m_new - cq))
            p_ref[slot] = p.astype(BF16)
            return m_new, alpha * l + jnp.sum(p, axis=0, keepdims=True), alpha

        def s3(t, slot, alpha):
            acc_ref[...] = alpha * acc_ref[...] + jnp.dot(vt_ref[t], p_ref[slot],
                                                          preferred_element_type=F32)

        acc_ref[...] = jnp.zeros_like(acc_ref)
        p_ref[1] = jnp.zeros(p_ref.shape[1:], BF16)
        s1(0, 0)

        def body(j, carry):
            m, l, alpha = carry
            t = 2 * j
            s3(jnp.maximum(t - 1, 0), 1, alpha)
            s1(t + 1, 1)
            m, l, alpha = s2(0, m, l)
            s3(t, 0, alpha)
            s1(t + 2, 0)
            return s2(1, m, l)

        m0 = jnp.full((1, tq), -jnp.inf, F32)
        l0 = jnp.zeros((1, tq), F32)
        m, l, alpha = lax.fori_loop(0, qi, body, (m0, l0, jnp.ones((1, tq), F32)))
        t = 2 * qi
        s3(jnp.maximum(t - 1, 0), 1, alpha)
        st = lax.dot_general(kx_ref[t + 1], qx[tk:tq], nt, preferred_element_type=F32)
        m, l, alpha = s2(0, m, l, key_offset=0)
        s3(t, 0, alpha)
        k_idx = lax.broadcasted_iota(jnp.int32, st.shape, 0)
        q_idx = lax.broadcasted_iota(jnp.int32, st.shape, 1)
        st = jnp.where(k_idx <= q_idx, st, -jnp.inf)
        m_hi, l_hi, cq_hi = m[:, tk:tq], l[:, tk:tq], cq[:, tk:tq]
        m_new = jnp.maximum(m_hi, jnp.max(st, axis=0, keepdims=True) + cq_hi)
        alpha_hi = jnp.exp2(m_hi - m_new)
        p = jnp.exp2(st - (m_new - cq_hi))
        l_hi = alpha_hi * l_hi + jnp.sum(p, axis=0, keepdims=True)
        acc_ref[:, tk:tq] = alpha_hi * acc_ref[:, tk:tq] + jnp.dot(
            vt_ref[t + 1], p.astype(BF16), preferred_element_type=F32)
        l = jnp.concatenate([l[:, 0:tk], l_hi], axis=1)
        o_ref[pl.ds(q0, tq), :] = jnp.transpose(acc_ref[...] / l).astype(o_ref.dtype)
        return carry_q

    lax.fori_loop(0, seq // tq, qblock, 0)


def _attn(qkv, cum, cum_t, batch, seq, heads, tq, tn_in):
    m = qkv.shape[1]
    hd = FOX_HEAD_DIM
    per_tile = tn_in // hd
    tiles = heads // per_tile
    n_q = seq // tq
    tk = tq // 2
    return pl.pallas_call(
        functools.partial(_attn_kernel, tq=tq, tk=tk, seq=seq),
        grid=(batch, heads),
        in_specs=[
            pl.BlockSpec((1, seq, hd), lambda b, h: (h // per_tile, b, h % per_tile)),
            pl.BlockSpec((1, seq, hd), lambda b, h: (tiles + h // per_tile, b, h % per_tile)),
            pl.BlockSpec((1, seq, hd), lambda b, h: (2 * tiles + h // per_tile, b, h % per_tile)),
            pl.BlockSpec((1, 8, n_q, tq), lambda b, h: (b, h // 8, 0, 0)),
            pl.BlockSpec((1, seq, LANES), lambda b, h: (b, 0, 0)),
        ],
        out_specs=pl.BlockSpec((seq, hd), lambda b, h: (b, h)),
        out_shape=jax.ShapeDtypeStruct((m, heads * hd), BF16),
        scratch_shapes=[
            pltpu.VMEM((seq // tk, tk, 2 * hd), BF16),
            pltpu.VMEM((seq // tk, hd, tk), BF16),
            pltpu.VMEM((2, tk, tq), F32),
            pltpu.VMEM((2, tk, tq), BF16),
            pltpu.VMEM((hd, tq), F32),
        ],
        compiler_params=_cparams(("parallel", "parallel")),
        name="attn",
    )(qkv, qkv, qkv, cum_t.reshape(batch, LANES, n_q, tq), cum)


def _pad_cols(a, n):
    return jnp.pad(a, ((0, 0), (0, n - a.shape[1])))


def kernel(x, c, norm_mix, norm_mlp, w_ada, b_ada, w_up, w_down, ssd_w_in, ssd_conv_w, ssd_conv_b,
           ssd_dt_bias, ssd_A_log, ssd_D, ssd_gnorm, ssd_w_out, fox_w_in, fox_b_f, fox_w_out,
           final_norm):
    batch, seq, d = x.shape
    m = batch * seq
    depth = w_ada.shape[0]
    d_inner = ssd_gnorm.shape[1]
    ssd_heads = ssd_dt_bias.shape[1]
    fox_heads = fox_b_f.shape[1]
    fox_width = fox_heads * FOX_HEAD_DIM
    assert d_inner == ssd_heads * SSD_HEAD_DIM and ssd_heads <= LANES and fox_heads <= LANES

    tm = min(1024, seq)
    tn_in = (d_inner // SSD_GROUP_WIDTH) * SSD_STATE
    tn_out = min(512, d)
    tf = min(512, w_up.shape[2])
    t_blk = min(1024, seq)
    tq = min(512, seq)
    assert fox_width % tn_in == 0 and d_inner % tn_in == 0

    rows = 8
    c_pad = jnp.pad(c, ((0, rows - batch), (0, 0)))
    mod = _ada(c_pad, w_ada, b_ada, tn=min(1024, N_MOD * d))[:, :batch]
    mod = mod.reshape(depth, batch, N_MOD, d).transpose(0, 2, 1, 3).reshape(depth * N_MOD * batch, 1, d)

    def mod_row(layer, which):
        return (layer * N_MOD + which) * batch

    plain = {"w_up": w_up, "w_down": w_down, "ssd_w_out": ssd_w_out, "fox_w_out": fox_w_out}
    h = x.reshape(m, d)
    for layer in range(depth):
        j = layer // 2
        g_mix = norm_mix[layer].reshape(1, d)
        g_mlp = norm_mlp[layer].reshape(1, d)
        if layer % 2 == 0:
            n_main = 2 * d_inner + 2 * (d_inner // SSD_GROUP_WIDTH) * SSD_STATE
            w_main, w_side = _cast_weights_t(jnp.transpose(ssd_w_in, (0, 2, 1)), n_main, tn_in)
            ride = tuple(plain.values()) if layer == 0 else ()
            zxbc, dt_raw, *cast = _in_proj(h, g_mix, mod, mod_row(layer, 0), mod_row(layer, 1),
                                           w_main, j, n_main, w_side, seq, tm, tn_in, ride_along=ride)
            if cast:
                plain = dict(zip(plain, cast))
            scales, acum, chunk_decay = _ssd_dt(dt_raw, _pad_cols(ssd_dt_bias[j].reshape(1, -1), LANES),
                                                _pad_cols(ssd_A_log[j].reshape(1, -1), LANES), t_blk)
            y = _ssd(zxbc, scales, acum, chunk_decay, ssd_conv_w[j], ssd_conv_b[j].reshape(1, -1),
                     jnp.repeat(ssd_D[j], SSD_HEAD_DIM).reshape(1, d_inner),
                     ssd_gnorm[j].reshape(1, d_inner), batch, seq, t_blk, tn_in)
            h = _out_proj(y, plain["ssd_w_out"], j, h, mod, mod_row(layer, 2), seq, tm, tn_out)
        else:
            w_main, w_side = _cast_weights_t(jnp.transpose(fox_w_in, (0, 2, 1)), 3 * fox_width, tn_in)
            qkv, f_logit = _in_proj(h, g_mix, mod, mod_row(layer, 0), mod_row(layer, 1),
                                    w_main, j, 3 * fox_width, w_side, seq, tm, tn_in,
                                    n_scaled=fox_width // tn_in, scale=FOX_HEAD_DIM ** -0.5 * LOG2E)
            cum, cum_t = _fcum(f_logit, _pad_cols(fox_b_f[j].reshape(1, -1), LANES), batch, seq)
            o = _attn(qkv, cum, cum_t, batch, seq, fox_heads, tq, tn_in)
            h = _out_proj(o, plain["fox_w_out"], j, h, mod, mod_row(layer, 2), seq, tm, tn_out)
        last = layer == depth - 1
        h = _mlp(h, g_mlp, mod, mod_row(layer, 3), mod_row(layer, 4), mod_row(layer, 5),
                 plain["w_up"], plain["w_down"], layer, final_norm.reshape(1, d),
                 seq, tm, tf, final_norm=last)
    return h.reshape(batch, seq, d)
```

```python
import functools

import jax
import jax.numpy as jnp
from jax import lax
from jax.experimental import pallas as pl
from jax.experimental.pallas import tpu as pltpu

F32 = jnp.float32
BF16 = jnp.bfloat16

EPS = 1e-6
N_MOD = 6
LANES = 128
SSD_HEAD_DIM = 64
SSD_STATE = 128
SSD_CONV = 4
SSD_HEADS_PER_GROUP = 8
SSD_GROUP_WIDTH = SSD_HEADS_PER_GROUP * SSD_HEAD_DIM
SSD_CHUNK = 128
FOX_HEAD_DIM = 128
LOG2E = 1.4426950408889634
VMEM_LIMIT = 56 * 1024 * 1024


def _cparams(sem):
    return pltpu.CompilerParams(dimension_semantics=sem, vmem_limit_bytes=VMEM_LIMIT)


def _silu(x):
    half = 0.5 * x
    return half + half * jnp.tanh(half)


def _softplus(x):
    return jnp.maximum(x, 0.0) + jnp.log1p(jnp.exp(-jnp.abs(x)))


def _split3(x):
    hi = x.astype(BF16)
    r1 = x - hi.astype(F32)
    mid = r1.astype(BF16)
    lo = (r1 - mid.astype(F32)).astype(BF16)
    return hi, mid, lo


def _cumsum_rows(x):
    n = x.shape[0]
    row = lax.broadcasted_iota(jnp.int32, x.shape, 0)
    k = 1
    while k < n:
        x = x + jnp.where(row >= k, pltpu.roll(x, k, 0), 0.0)
        k *= 2
    return x


def _norm_mod(x, g, shift, scale):
    ms = jnp.mean(x * x, axis=-1, keepdims=True)
    return x * lax.rsqrt(ms + EPS) * (g * (1.0 + scale)) + shift


def _cast_t_kernel(w_ref, o_ref):
    w = w_ref[0]
    pad = o_ref.shape[2] - w.shape[0]
    if pad:
        w = jnp.concatenate([w, jnp.zeros((pad, w.shape[1]), F32)], axis=0)
    o_ref[0] = jnp.transpose(w).astype(BF16)


def _cast_weights_t(w_t, c_main, cols_blk):
    layers, c, k = w_t.shape
    side = c - c_main
    assert 0 < side <= LANES and c_main % side == 0 and side % 8 == 0 and c_main % cols_blk == 0
    main = pl.pallas_call(
        _cast_t_kernel,
        grid=(layers, c_main // cols_blk),
        in_specs=[pl.BlockSpec((1, cols_blk, k), lambda l, i: (l, i, 0))],
        out_specs=pl.BlockSpec((1, k, cols_blk), lambda l, i: (l, 0, i)),
        out_shape=jax.ShapeDtypeStruct((layers, k, c_main), BF16),
        compiler_params=_cparams(("parallel", "parallel")),
        name="cast_t",
    )(w_t)
    tail = pl.pallas_call(
        _cast_t_kernel,
        grid=(layers,),
        in_specs=[pl.BlockSpec((1, side, k), lambda l: (l, c_main // side, 0))],
        out_specs=pl.BlockSpec((1, k, LANES), lambda l: (l, 0, 0)),
        out_shape=jax.ShapeDtypeStruct((layers, k, LANES), BF16),
        compiler_params=_cparams(("parallel",)),
        name="cast_t_side",
    )(w_t)
    return main, tail


def _ada_kernel(c_ref, w_ref, b_ref, o_ref):
    cond = _silu(c_ref[...])
    o_ref[0] = jnp.dot(cond.astype(BF16), w_ref[0].astype(BF16),
                       preferred_element_type=F32) + b_ref[0]


def _ada(c_pad, w_ada, b_ada, tn):
    depth, d, n = w_ada.shape
    rows = c_pad.shape[0]
    return pl.pallas_call(
        _ada_kernel,
        grid=(depth, n // tn),
        in_specs=[
            pl.BlockSpec((rows, d), lambda l, j: (0, 0)),
            pl.BlockSpec((1, d, tn), lambda l, j: (l, 0, j)),
            pl.BlockSpec((1, 1, tn), lambda l, j: (l, 0, j)),
        ],
        out_specs=pl.BlockSpec((1, rows, tn), lambda l, j: (l, 0, j)),
        out_shape=jax.ShapeDtypeStruct((depth, rows, n), F32),
        compiler_params=_cparams(("parallel", "parallel")),
        name="ada",
    )(c_pad, w_ada, b_ada.reshape(depth, 1, n))


def _in_proj_kernel(x_ref, g_ref, sh_ref, sc_ref, w_ref, wx_ref, *rest, n_scaled, scale, n_ride):
    ride_in = rest[:n_ride]
    o_ref, ox_ref = rest[n_ride:n_ride + 2]
    ride_out = rest[n_ride + 2:2 * n_ride + 2]
    u_ref = rest[-1]
    j = pl.program_id(1)

    @pl.when(j == 0)
    def _():
        u = _norm_mod(x_ref[...], g_ref[...], sh_ref[0], sc_ref[0]).astype(BF16)
        u_ref[...] = u
        ox_ref[...] = jnp.dot(u, wx_ref[0], preferred_element_type=F32)

    acc = jnp.dot(u_ref[...], w_ref[0], preferred_element_type=F32)
    if n_scaled:
        acc = acc * jnp.where(j < n_scaled, scale, 1.0)
    o_ref[0] = acc.astype(o_ref.dtype)
    for src, dst in zip(ride_in, ride_out):
        dst[...] = src[...].astype(BF16)


def _ride_rows(w, steps):
    layers, r, _ = w.shape
    for rb in range(16, r + 1, 16):
        if r % rb == 0 and layers * (r // rb) <= steps:
            return rb
    raise ValueError("weight too large to ride along")


def _in_proj(h, g, mod, sh_row, sc_row, w, w_layer, n, wx, seq, tm, tn, n_scaled=0, scale=1.0,
             ride_along=()):
    m, d = h.shape
    per_b = seq // tm
    n_j = n // tn
    steps = (m // tm) * n_j
    ride_specs, ride_shapes = [], []
    for wr in ride_along:
        rb = _ride_rows(wr, steps)
        per_layer = wr.shape[1] // rb
        last = wr.shape[0] * per_layer - 1

        def slab(i, j, per_layer=per_layer, last=last):
            s = jnp.minimum(i * n_j + j, last)
            return (s // per_layer, s % per_layer, 0)

        ride_specs.append(pl.BlockSpec((1, rb, wr.shape[2]), slab))
        ride_shapes.append(jax.ShapeDtypeStruct(wr.shape, BF16))
    return pl.pallas_call(
        functools.partial(_in_proj_kernel, n_scaled=n_scaled, scale=scale, n_ride=len(ride_along)),
        grid=(m // tm, n_j),
        in_specs=[
            pl.BlockSpec((tm, d), lambda i, j: (i, 0)),
            pl.BlockSpec((1, d), lambda i, j: (0, 0)),
            pl.BlockSpec((1, 1, d), lambda i, j: (sh_row + i // per_b, 0, 0)),
            pl.BlockSpec((1, 1, d), lambda i, j: (sc_row + i // per_b, 0, 0)),
            pl.BlockSpec((1, d, tn), lambda i, j: (w_layer, 0, j)),
            pl.BlockSpec((1, d, LANES), lambda i, j: (w_layer, 0, 0)),
        ] + ride_specs,
        out_specs=[
            pl.BlockSpec((1, tm, tn), lambda i, j: (j, i, 0)),
            pl.BlockSpec((tm, LANES), lambda i, j: (i, 0)),
        ] + ride_specs,
        out_shape=[
            jax.ShapeDtypeStruct((n_j, m, tn), BF16),
            jax.ShapeDtypeStruct((m, LANES), F32),
        ] + ride_shapes,
        scratch_shapes=[pltpu.VMEM((tm, d), BF16)],
        compiler_params=_cparams(("arbitrary", "arbitrary")),
        name="in_proj",
    )(h, g, mod, mod, w, wx, *ride_along)


def _out_proj_kernel(a_ref, w_ref, h_ref, gate_ref, o_ref):
    acc = jnp.dot(a_ref[...], w_ref[0], preferred_element_type=F32)
    o_ref[...] = h_ref[...] + gate_ref[0] * acc


def _out_proj(a, w, w_layer, h, mod, gate_row, seq, tm, tn):
    m, k = a.shape
    d = w.shape[2]
    per_b = seq // tm
    return pl.pallas_call(
        _out_proj_kernel,
        grid=(m // tm, d // tn),
        in_specs=[
            pl.BlockSpec((tm, k), lambda i, j: (i, 0)),
            pl.BlockSpec((1, k, tn), lambda i, j: (w_layer, 0, j)),
            pl.BlockSpec((tm, tn), lambda i, j: (i, j)),
            pl.BlockSpec((1, 1, tn), lambda i, j: (gate_row + i // per_b, 0, j)),
        ],
        out_specs=pl.BlockSpec((tm, tn), lambda i, j: (i, j)),
        out_shape=jax.ShapeDtypeStruct((m, d), F32),
        compiler_params=_cparams(("parallel", "arbitrary")),
        name="out_proj",
    )(a, w, h, mod)


def _mlp_kernel(h_ref, g_ref, sh_ref, sc_ref, gate_ref, wu_ref, wd_ref, fn_ref, o_ref, u_ref,
                *, final_norm):
    f = pl.program_id(1)

    @pl.when(f == 0)
    def _():
        u_ref[...] = _norm_mod(h_ref[...], g_ref[...], sh_ref[0], sc_ref[0]).astype(BF16)
        o_ref[...] = jnp.zeros_like(o_ref)

    a = jnp.dot(u_ref[...], wu_ref[0], preferred_element_type=F32)
    a = jnp.square(jnp.maximum(a, 0.0)).astype(BF16)
    o_ref[...] += jnp.dot(a, wd_ref[0], preferred_element_type=F32)

    @pl.when(f == pl.num_programs(1) - 1)
    def _():
        y = h_ref[...] + gate_ref[0] * o_ref[...]
        if final_norm:
            ms = jnp.mean(y * y, axis=-1, keepdims=True)
            y = y * lax.rsqrt(ms + EPS) * fn_ref[...]
        o_ref[...] = y


def _mlp(h, g, mod, sh_row, sc_row, gate_row, w_up, w_down, layer, fn, seq, tm, tf, final_norm):
    m, d = h.shape
    dff = w_up.shape[2]
    per_b = seq // tm
    return pl.pallas_call(
        functools.partial(_mlp_kernel, final_norm=final_norm),
        grid=(m // tm, dff // tf),
        in_specs=[
            pl.BlockSpec((tm, d), lambda i, f: (i, 0)),
            pl.BlockSpec((1, d), lambda i, f: (0, 0)),
            pl.BlockSpec((1, 1, d), lambda i, f: (sh_row + i // per_b, 0, 0)),
            pl.BlockSpec((1, 1, d), lambda i, f: (sc_row + i // per_b, 0, 0)),
            pl.BlockSpec((1, 1, d), lambda i, f: (gate_row + i // per_b, 0, 0)),
            pl.BlockSpec((1, d, tf), lambda i, f: (layer, 0, f)),
            pl.BlockSpec((1, tf, d), lambda i, f: (layer, f, 0)),
            pl.BlockSpec((1, d), lambda i, f: (0, 0)),
        ],
        out_specs=pl.BlockSpec((tm, d), lambda i, f: (i, 0)),
        out_shape=jax.ShapeDtypeStruct((m, d), F32),
        scratch_shapes=[pltpu.VMEM((tm, d), BF16)],
        compiler_params=_cparams(("parallel", "arbitrary")),
        name="mlp",
    )(h, g, mod, mod, mod, w_up, w_down, fn)


def _ssd_dt_kernel(dt_ref, dtb_ref, alog_ref, sc_ref, acum_ref, cd_ref):
    ck = SSD_CHUNK
    a_neg = -jnp.exp(alog_ref[...])
    for c in range(dt_ref.shape[0] // ck):
        rows = slice(c * ck, (c + 1) * ck)
        dt = _softplus(dt_ref[rows, :] + dtb_ref[...])
        acum = _cumsum_rows(dt * a_neg)
        a_last = acum[ck - 1:ck, :]
        sc_ref[0, rows, :] = dt
        sc_ref[1, rows, :] = dt * jnp.exp(a_last - acum)
        sc_ref[2, rows, :] = jnp.exp(acum)
        acum_ref[rows, :] = acum
        cd_ref[c] = jnp.broadcast_to(jnp.exp(a_last), (8, LANES))


def _ssd_dt(dt_raw, dtb, alog, blk):
    m = dt_raw.shape[0]
    per = blk // SSD_CHUNK
    return pl.pallas_call(
        _ssd_dt_kernel,
        grid=(m // blk,),
        in_specs=[
            pl.BlockSpec((blk, LANES), lambda i: (i, 0)),
            pl.BlockSpec((1, LANES), lambda i: (0, 0)),
            pl.BlockSpec((1, LANES), lambda i: (0, 0)),
        ],
        out_specs=[
            pl.BlockSpec((3, blk, LANES), lambda i: (0, i, 0)),
            pl.BlockSpec((blk, LANES), lambda i: (i, 0)),
            pl.BlockSpec((per, 8, LANES), lambda i: (i, 0, 0)),
        ],
        out_shape=[
            jax.ShapeDtypeStruct((3, m, LANES), F32),
            jax.ShapeDtypeStruct((m, LANES), F32),
            jax.ShapeDtypeStruct((m // SSD_CHUNK, 8, LANES), F32),
        ],
        compiler_params=_cparams(("parallel",)),
        name="ssd_dt",
    )(dt_raw, dtb, alog)


def _ssd_kernel(z_ref, x_ref, b_ref, c_ref, sc_ref, acum_ref, cd_ref, wx_ref, wb_ref, wc_ref,
                bx_ref, bb_ref, bc_ref, d_ref, gn_ref, o_ref, state_ref, xbc_ref, *, t_blk):
    g = pl.program_id(1)
    t = pl.program_id(2)
    ck = SSD_CHUNK
    gw = SSD_GROUP_WIDTH
    cw = gw + 2 * SSD_STATE

    @pl.when(t == 0)
    def _():
        state_ref[...] = jnp.zeros_like(state_ref)
        xbc_ref[0:ck, :] = jnp.zeros((ck, cw), BF16)

    xbc_ref[ck:ck + t_blk, 0:gw] = x_ref[0]
    xbc_ref[ck:ck + t_blk, gw:gw + SSD_STATE] = b_ref[0]
    xbc_ref[ck:ck + t_blk, gw + SSD_STATE:cw] = c_ref[0]
    conv_w = jnp.concatenate([wx_ref[...], wb_ref[...], wc_ref[...]], axis=1)
    conv_b = jnp.concatenate([bx_ref[...], bb_ref[...], bc_ref[...]], axis=1)

    sh_r = lax.broadcasted_iota(jnp.int32, ((SSD_CONV - 1) * ck, 2 * ck), 0)
    sh_c = lax.broadcasted_iota(jnp.int32, ((SSD_CONV - 1) * ck, 2 * ck), 1)
    shift = (sh_c == (sh_r & (ck - 1)) + (ck - (SSD_CONV - 1)) + (sh_r >> 7)).astype(BF16)
    assert ck == 128

    row = lax.broadcasted_iota(jnp.int32, (ck, ck), 0)
    col = lax.broadcasted_iota(jnp.int32, (ck, ck), 1)
    causal = row >= col
    low_half = col < SSD_HEAD_DIM
    e_h = lax.broadcasted_iota(jnp.int32, (LANES, gw), 0)
    e_c = lax.broadcasted_iota(jnp.int32, (LANES, gw), 1)
    expand = (e_h == g * SSD_HEADS_PER_GROUP + (e_c >> 6)).astype(BF16)
    expand2 = jnp.concatenate([expand, expand], axis=0)
    expand3 = jnp.concatenate([expand2, expand], axis=0)
    to_lane0 = (LANES - g * SSD_HEADS_PER_GROUP) % LANES

    nt = (((1,), (1,)), ((), ()))
    tn = (((0,), (0,)), ((), ()))

    def chunk(c, carry):
        r0 = pl.multiple_of(c * ck, ck)
        window = xbc_ref[pl.ds(r0, 2 * ck), :]
        back = jnp.dot(shift, window, preferred_element_type=F32)
        acc = conv_b + conv_w[SSD_CONV - 1:SSD_CONV, :] * window[ck:2 * ck].astype(F32)
        for k in range(SSD_CONV - 1):
            acc = acc + conv_w[k:k + 1, :] * back[k * ck:(k + 1) * ck]
        conv = _silu(acc)
        xs = conv[:, 0:gw]
        bm = conv[:, gw:gw + SSD_STATE].astype(BF16)
        cm = conv[:, gw + SSD_STATE:cw].astype(BF16)

        scales = jnp.concatenate([sc_ref[i, pl.ds(r0, ck), :] for i in range(3)], axis=0)
        acum = acum_ref[pl.ds(r0, ck), :]
        ex = jnp.dot(jnp.concatenate(_split3(scales)[:2], axis=1), expand2,
                     preferred_element_type=F32)
        dt_x = ex[0:ck]
        wend_x = ex[ck:2 * ck]
        ea_x = ex[2 * ck:3 * ck]
        cd_x = jnp.dot(jnp.concatenate(_split3(cd_ref[c]), axis=1), expand3,
                       preferred_element_type=F32)[0:1]
        acum_g = pltpu.roll(acum, to_lane0, 1)
        acum_t = jnp.transpose(acum_g)

        scores = lax.dot_general(cm, bm, nt, preferred_element_type=F32)
        xdt = (xs * dt_x).astype(BF16)
        y_pairs = []
        for p in range(SSD_HEADS_PER_GROUP // 2):
            x_pair = xdt[:, p * LANES:(p + 1) * LANES]
            rhs = jnp.concatenate([jnp.where(low_half, x_pair, jnp.zeros_like(x_pair)),
                                   jnp.where(low_half, jnp.zeros_like(x_pair), x_pair)], axis=0)
            lhs = []
            for j in (2 * p, 2 * p + 1):
                a_col = jnp.broadcast_to(acum_g[:, j:j + 1], (ck, ck))
                seg = a_col - acum_t[j:j + 1, :]
                decay = jnp.exp(jnp.where(causal, seg, -jnp.inf))
                lhs.append((scores * decay).astype(BF16))
            y_pairs.append(jnp.dot(jnp.concatenate(lhs, axis=1), rhs, preferred_element_type=F32))
        y = jnp.concatenate(y_pairs, axis=1)

        state = state_ref[...]
        y = y + jnp.dot(cm, state.astype(BF16), preferred_element_type=F32) * ea_x
        state_ref[...] = state * cd_x + lax.dot_general(
            bm, (xs * wend_x).astype(BF16), tn, preferred_element_type=F32)

        y = y + xs * d_ref[...]
        y = y * _silu(z_ref[0, pl.ds(r0, ck), :].astype(F32))
        ms = jnp.mean(y * y, axis=-1, keepdims=True)
        y = y * lax.rsqrt(ms + EPS) * gn_ref[...]
        o_ref[pl.ds(r0, ck), :] = y.astype(o_ref.dtype)
        return carry

    lax.fori_loop(0, t_blk // ck, chunk, 0, unroll=2)
    xbc_ref[0:ck, :] = xbc_ref[t_blk:t_blk + ck, :]


def _ssd(zxbc, scales, acum, chunk_decay, conv_w, conv_b, d_x, gnorm, batch, seq, t_blk, tn_in):
    m = acum.shape[0]
    gw = SSD_GROUP_WIDTH
    d_inner = gnorm.shape[1]
    groups = d_inner // gw
    n_t = seq // t_blk
    per_tile = tn_in // gw
    x_tile0 = d_inner // tn_in
    b_tile = 2 * d_inner // tn_in
    c_tile = b_tile + 1
    assert groups * SSD_STATE == tn_in and tn_in % gw == 0
    xw0 = 0
    bw0 = d_inner // SSD_STATE
    cw0 = bw0 + groups

    def rows(b, t):
        return b * n_t + t

    return pl.pallas_call(
        functools.partial(_ssd_kernel, t_blk=t_blk),
        grid=(batch, groups, n_t),
        in_specs=[
            pl.BlockSpec((1, t_blk, gw), lambda b, g, t: (g // per_tile, rows(b, t), g % per_tile)),
            pl.BlockSpec((1, t_blk, gw), lambda b, g, t: (x_tile0 + g // per_tile, rows(b, t), g % per_tile)),
            pl.BlockSpec((1, t_blk, SSD_STATE), lambda b, g, t: (b_tile, rows(b, t), g)),
            pl.BlockSpec((1, t_blk, SSD_STATE), lambda b, g, t: (c_tile, rows(b, t), g)),
            pl.BlockSpec((3, t_blk, LANES), lambda b, g, t: (0, rows(b, t), 0)),
            pl.BlockSpec((t_blk, LANES), lambda b, g, t: (rows(b, t), 0)),
            pl.BlockSpec((t_blk // SSD_CHUNK, 8, LANES), lambda b, g, t: (rows(b, t), 0, 0)),
            pl.BlockSpec((SSD_CONV, gw), lambda b, g, t: (0, xw0 + g)),
            pl.BlockSpec((SSD_CONV, SSD_STATE), lambda b, g, t: (0, bw0 + g)),
            pl.BlockSpec((SSD_CONV, SSD_STATE), lambda b, g, t: (0, cw0 + g)),
            pl.BlockSpec((1, gw), lambda b, g, t: (0, xw0 + g)),
            pl.BlockSpec((1, SSD_STATE), lambda b, g, t: (0, bw0 + g)),
            pl.BlockSpec((1, SSD_STATE), lambda b, g, t: (0, cw0 + g)),
            pl.BlockSpec((1, gw), lambda b, g, t: (0, g)),
            pl.BlockSpec((1, gw), lambda b, g, t: (0, g)),
        ],
        out_specs=pl.BlockSpec((t_blk, gw), lambda b, g, t: (rows(b, t), g)),
        out_shape=jax.ShapeDtypeStruct((m, d_inner), BF16),
        scratch_shapes=[
            pltpu.VMEM((SSD_STATE, gw), F32),
            pltpu.VMEM((t_blk + SSD_CHUNK, gw + 2 * SSD_STATE), BF16),
        ],
        compiler_params=_cparams(("parallel", "parallel", "arbitrary")),
        name="ssd",
    )(zxbc, zxbc, zxbc, zxbc, scales, acum, chunk_decay, conv_w, conv_w, conv_w, conv_b, conv_b, conv_b,
      d_x, gnorm)


def _fcum_kernel(f_ref, b_ref, o_ref, ot_ref):
    x = f_ref[...] + b_ref[...]
    cum = _cumsum_rows(-_softplus(-x)) * LOG2E
    o_ref[0] = cum
    cum_t = jnp.transpose(cum)
    tq = ot_ref.shape[3]
    for i in range(ot_ref.shape[2]):
        ot_ref[0, :, i, :] = cum_t[:, i * tq:(i + 1) * tq]


def _fcum(f_logit, b_f, batch, seq, tq):
    return pl.pallas_call(
        _fcum_kernel,
        grid=(batch,),
        in_specs=[
            pl.BlockSpec((seq, LANES), lambda b: (b, 0)),
            pl.BlockSpec((1, LANES), lambda b: (0, 0)),
        ],
        out_specs=[
            pl.BlockSpec((1, seq, LANES), lambda b: (b, 0, 0)),
            pl.BlockSpec((1, LANES, seq // tq, tq), lambda b: (b, 0, 0, 0)),
        ],
        out_shape=[
            jax.ShapeDtypeStruct((batch, seq, LANES), F32),
            jax.ShapeDtypeStruct((batch, LANES, seq // tq, tq), F32),
        ],
        compiler_params=_cparams(("parallel",)),
        name="fcum",
    )(f_logit, b_f)


def _key_bias_columns(cum, head):
    r = lax.broadcasted_iota(jnp.int32, (LANES, LANES), 0)
    c = lax.broadcasted_iota(jnp.int32, (LANES, LANES), 1)
    out = None
    for i, piece in enumerate(_split3(cum)):
        sel = ((r == head) & (c == i)).astype(BF16)
        term = jnp.dot(piece, sel, preferred_element_type=F32)
        out = term if out is None else out + term
    return (-out).astype(BF16)


def _attn_kernel(q_ref, k_ref, v_ref, cq_ref, ck_ref, o_ref, kx_ref, vt_ref, s_ref, p_ref, acc_ref,
                 *, tq, tk, seq):
    head = pl.program_id(1)
    hd = FOX_HEAD_DIM
    nt = (((1,), (1,)), ((), ()))
    assert tq == 2 * tk

    for rc in range(seq // tk):
        r0 = rc * tk
        kx_ref[rc, :, 0:hd] = k_ref[0, r0:r0 + tk, :]
        kx_ref[rc, :, hd:2 * hd] = _key_bias_columns(ck_ref[0, r0:r0 + tk, :], head)
        vt_ref[rc] = jnp.transpose(v_ref[0, r0:r0 + tk, :].astype(F32)).astype(BF16)

    lane = lax.broadcasted_iota(jnp.int32, (tq, hd), 1)

    def queries(qb):
        rows = q_ref[0, pl.ds(pl.multiple_of(qb * tq, tq), tq), :]
        return jnp.concatenate([rows, (lane < 3).astype(BF16)], axis=1)

    def first_scores(qb):
        s_ref[0] = lax.dot_general(kx_ref[0], queries(qb), nt, preferred_element_type=F32)

    first_scores(0)

    def qblock(qi, carry_q):
        q0 = pl.multiple_of(qi * tq, tq)
        qx = queries(qi)
        cq = cq_ref[0, head % 8, pl.ds(qi, 1), :]

        def s1(t, slot):
            s_ref[slot] = lax.dot_general(kx_ref[t], qx, nt, preferred_element_type=F32)

        def s2(slot, m, l, key_offset=None):
            st = s_ref[slot]
            if key_offset is not None:
                k_idx = lax.broadcasted_iota(jnp.int32, st.shape, 0) + key_offset
                q_idx = lax.broadcasted_iota(jnp.int32, st.shape, 1)
                st = jnp.where(k_idx <= q_idx, st, -jnp.inf)
            m_new = jnp.maximum(m, jnp.max(st, axis=0, keepdims=True) + cq)
            alpha = jnp.exp2(m - m_new)
            p = jnp.exp2(st - (m_new - cq))
            p_ref[slot] = p.astype(BF16)
            return m_new, alpha * l + jnp.sum(p, axis=0, keepdims=True), alpha

        def s3(t, slot, alpha):
            acc_ref[...] = alpha * acc_ref[...] + jnp.dot(vt_ref[t], p_ref[slot],
                                                          preferred_element_type=F32)

        acc_ref[...] = jnp.zeros_like(acc_ref)
        p_ref[1] = jnp.zeros(p_ref.shape[1:], BF16)

        def body(j, carry):
            m, l, alpha = carry
            t = 2 * j
            s3(jnp.maximum(t - 1, 0), 1, alpha)
            s1(t + 1, 1)
            m, l, alpha = s2(0, m, l)
            s3(t, 0, alpha)
            s1(t + 2, 0)
            return s2(1, m, l)

        m0 = jnp.full((1, tq), -jnp.inf, F32)
        l0 = jnp.zeros((1, tq), F32)
        m, l, alpha = lax.fori_loop(0, qi, body, (m0, l0, jnp.ones((1, tq), F32)))
        t = 2 * qi
        s3(jnp.maximum(t - 1, 0), 1, alpha)
        st = lax.dot_general(kx_ref[t + 1], qx[tk:tq], nt, preferred_element_type=F32)
        m, l, alpha = s2(0, m, l, key_offset=0)
        s3(t, 0, alpha)
        k_idx = lax.broadcasted_iota(jnp.int32, st.shape, 0)
        q_idx = lax.broadcasted_iota(jnp.int32, st.shape, 1)
        st = jnp.where(k_idx <= q_idx, st, -jnp.inf)
        m_hi, l_hi, cq_hi = m[:, tk:tq], l[:, tk:tq], cq[:, tk:tq]
        m_new = jnp.maximum(m_hi, jnp.max(st, axis=0, keepdims=True) + cq_hi)
        alpha_hi = jnp.exp2(m_hi - m_new)
        p = jnp.exp2(st - (m_new - cq_hi))
        l_hi = alpha_hi * l_hi + jnp.sum(p, axis=0, keepdims=True)
        acc_ref[:, tk:tq] = alpha_hi * acc_ref[:, tk:tq] + jnp.dot(
            vt_ref[t + 1], p.astype(BF16), preferred_element_type=F32)
        l = jnp.concatenate([l[:, 0:tk], l_hi], axis=1)
        first_scores(jnp.minimum(qi + 1, seq // tq - 1))
        o_ref[pl.ds(q0, tq), :] = jnp.transpose(acc_ref[...] / l).astype(o_ref.dtype)
        return carry_q

    lax.fori_loop(0, seq // tq, qblock, 0)


def _attn(qkv, cum, cum_t, batch, seq, heads, tq, tn_in):
    m = qkv.shape[1]
    hd = FOX_HEAD_DIM
    per_tile = tn_in // hd
    tiles = heads // per_tile
    n_q = seq // tq
    tk = tq // 2
    return pl.pallas_call(
        functools.partial(_attn_kernel, tq=tq, tk=tk, seq=seq),
        grid=(batch, heads),
        in_specs=[
            pl.BlockSpec((1, seq, hd), lambda b, h: (h // per_tile, b, h % per_tile)),
            pl.BlockSpec((1, seq, hd), lambda b, h: (tiles + h // per_tile, b, h % per_tile)),
            pl.BlockSpec((1, seq, hd), lambda b, h: (2 * tiles + h // per_tile, b, h % per_tile)),
            pl.BlockSpec((1, 8, n_q, tq), lambda b, h: (b, h // 8, 0, 0)),
            pl.BlockSpec((1, seq, LANES), lambda b, h: (b, 0, 0)),
        ],
        out_specs=pl.BlockSpec((seq, hd), lambda b, h: (b, h)),
        out_shape=jax.ShapeDtypeStruct((m, heads * hd), BF16),
        scratch_shapes=[
            pltpu.VMEM((seq // tk, tk, 2 * hd), BF16),
            pltpu.VMEM((seq // tk, hd, tk), BF16),
            pltpu.VMEM((2, tk, tq), F32),
            pltpu.VMEM((2, tk, tq), BF16),
            pltpu.VMEM((hd, tq), F32),
        ],
        compiler_params=_cparams(("parallel", "parallel")),
        name="attn",
    )(qkv, qkv, qkv, cum_t, cum)


def _pad_cols(a, n):
    return jnp.pad(a, ((0, 0), (0, n - a.shape[1])))


def kernel(x, c, norm_mix, norm_mlp, w_ada, b_ada, w_up, w_down, ssd_w_in, ssd_conv_w, ssd_conv_b,
           ssd_dt_bias, ssd_A_log, ssd_D, ssd_gnorm, ssd_w_out, fox_w_in, fox_b_f, fox_w_out,
           final_norm):
    batch, seq, d = x.shape
    m = batch * seq
    depth = w_ada.shape[0]
    d_inner = ssd_gnorm.shape[1]
    ssd_heads = ssd_dt_bias.shape[1]
    fox_heads = fox_b_f.shape[1]
    fox_width = fox_heads * FOX_HEAD_DIM
    assert d_inner == ssd_heads * SSD_HEAD_DIM and ssd_heads <= LANES and fox_heads <= LANES

    tm = min(1024, seq)
    tn_in = (d_inner // SSD_GROUP_WIDTH) * SSD_STATE
    tn_out = min(512, d)
    tf = min(512, w_up.shape[2])
    t_blk = min(1024, seq)
    tq = min(512, seq)
    assert fox_width % tn_in == 0 and d_inner % tn_in == 0

    rows = 8
    c_pad = jnp.pad(c, ((0, rows - batch), (0, 0)))
    mod = _ada(c_pad, w_ada, b_ada, tn=min(1024, N_MOD * d))[:, :batch]
    mod = mod.reshape(depth, batch, N_MOD, d).transpose(0, 2, 1, 3).reshape(depth * N_MOD * batch, 1, d)

    def mod_row(layer, which):
        return (layer * N_MOD + which) * batch

    plain = {"w_up": w_up, "w_down": w_down, "ssd_w_out": ssd_w_out, "fox_w_out": fox_w_out}
    h = x.reshape(m, d)
    for layer in range(depth):
        j = layer // 2
        g_mix = norm_mix[layer].reshape(1, d)
        g_mlp = norm_mlp[layer].reshape(1, d)
        if layer % 2 == 0:
            n_main = 2 * d_inner + 2 * (d_inner // SSD_GROUP_WIDTH) * SSD_STATE
            w_main, w_side = _cast_weights_t(jnp.transpose(ssd_w_in, (0, 2, 1)), n_main, tn_in)
            ride = tuple(plain.values()) if layer == 0 else ()
            zxbc, dt_raw, *cast = _in_proj(h, g_mix, mod, mod_row(layer, 0), mod_row(layer, 1),
                                           w_main, j, n_main, w_side, seq, tm, tn_in, ride_along=ride)
            if cast:
                plain = dict(zip(plain, cast))
            scales, acum, chunk_decay = _ssd_dt(dt_raw, _pad_cols(ssd_dt_bias[j].reshape(1, -1), LANES),
                                                _pad_cols(ssd_A_log[j].reshape(1, -1), LANES), t_blk)
            y = _ssd(zxbc, scales, acum, chunk_decay, ssd_conv_w[j], ssd_conv_b[j].reshape(1, -1),
                     jnp.repeat(ssd_D[j], SSD_HEAD_DIM).reshape(1, d_inner),
                     ssd_gnorm[j].reshape(1, d_inner), batch, seq, t_blk, tn_in)
            h = _out_proj(y, plain["ssd_w_out"], j, h, mod, mod_row(layer, 2), seq, tm, tn_out)
        else:
            w_main, w_side = _cast_weights_t(jnp.transpose(fox_w_in, (0, 2, 1)), 3 * fox_width, tn_in)
            qkv, f_logit = _in_proj(h, g_mix, mod, mod_row(layer, 0), mod_row(layer, 1),
                                    w_main, j, 3 * fox_width, w_side, seq, tm, tn_in,
                                    n_scaled=fox_width // tn_in, scale=FOX_HEAD_DIM ** -0.5 * LOG2E)
            cum, cum_t = _fcum(f_logit, _pad_cols(fox_b_f[j].reshape(1, -1), LANES), batch, seq, tq)
            o = _attn(qkv, cum, cum_t, batch, seq, fox_heads, tq, tn_in)
            h = _out_proj(o, plain["fox_w_out"], j, h, mod, mod_row(layer, 2), seq, tm, min(2 * tn_out, d))
        last = layer == depth - 1
        h = _mlp(h, g_mlp, mod, mod_row(layer, 3), mod_row(layer, 4), mod_row(layer, 5),
                 plain["w_up"], plain["w_down"], layer, final_norm.reshape(1, d),
                 seq, tm, tf, final_norm=last)
    return h.reshape(batch, seq, d)
```

```python
import functools

import jax
import jax.numpy as jnp
from jax import lax
from jax.experimental import pallas as pl
from jax.experimental.pallas import tpu as pltpu

F32 = jnp.float32
BF16 = jnp.bfloat16

EPS = 1e-6
N_MOD = 6
LANES = 128
SSD_HEAD_DIM = 64
SSD_STATE = 128
SSD_CONV = 4
SSD_HEADS_PER_GROUP = 8
SSD_GROUP_WIDTH = SSD_HEADS_PER_GROUP * SSD_HEAD_DIM
SSD_CHUNK = 128
FOX_HEAD_DIM = 128
LOG2E = 1.4426950408889634
VMEM_LIMIT = 56 * 1024 * 1024


def _cparams(sem):
    return pltpu.CompilerParams(dimension_semantics=sem, vmem_limit_bytes=VMEM_LIMIT)


def _silu(x):
    half = 0.5 * x
    return half + half * jnp.tanh(half)


def _softplus(x):
    return jnp.maximum(x, 0.0) + jnp.log1p(jnp.exp(-jnp.abs(x)))


def _split3(x):
    hi = x.astype(BF16)
    r1 = x - hi.astype(F32)
    mid = r1.astype(BF16)
    lo = (r1 - mid.astype(F32)).astype(BF16)
    return hi, mid, lo


def _cumsum_rows(x):
    n = x.shape[0]
    row = lax.broadcasted_iota(jnp.int32, x.shape, 0)
    k = 1
    while k < n:
        x = x + jnp.where(row >= k, pltpu.roll(x, k, 0), 0.0)
        k *= 2
    return x


def _norm_mod(x, g, shift, scale):
    ms = jnp.mean(x * x, axis=-1, keepdims=True)
    return x * lax.rsqrt(ms + EPS) * (g * (1.0 + scale)) + shift


def _cast_t_kernel(w_ref, o_ref):
    w = w_ref[0]
    pad = o_ref.shape[2] - w.shape[0]
    if pad:
        w = jnp.concatenate([w, jnp.zeros((pad, w.shape[1]), F32)], axis=0)
    o_ref[0] = jnp.transpose(w).astype(BF16)


def _cast_weights_t(w_t, c_main, cols_blk):
    layers, c, k = w_t.shape
    side = c - c_main
    assert 0 < side <= LANES and c_main % side == 0 and side % 8 == 0 and c_main % cols_blk == 0
    main = pl.pallas_call(
        _cast_t_kernel,
        grid=(layers, c_main // cols_blk),
        in_specs=[pl.BlockSpec((1, cols_blk, k), lambda l, i: (l, i, 0))],
        out_specs=pl.BlockSpec((1, k, cols_blk), lambda l, i: (l, 0, i)),
        out_shape=jax.ShapeDtypeStruct((layers, k, c_main), BF16),
        compiler_params=_cparams(("parallel", "parallel")),
        name="cast_t",
    )(w_t)
    tail = pl.pallas_call(
        _cast_t_kernel,
        grid=(layers,),
        in_specs=[pl.BlockSpec((1, side, k), lambda l: (l, c_main // side, 0))],
        out_specs=pl.BlockSpec((1, k, LANES), lambda l: (l, 0, 0)),
        out_shape=jax.ShapeDtypeStruct((layers, k, LANES), BF16),
        compiler_params=_cparams(("parallel",)),
        name="cast_t_side",
    )(w_t)
    return main, tail


def _ada_kernel(c_ref, w_ref, b_ref, o_ref):
    cond = _silu(c_ref[...])
    o_ref[0] = jnp.dot(cond.astype(BF16), w_ref[0].astype(BF16),
                       preferred_element_type=F32) + b_ref[0]


def _ada(c_pad, w_ada, b_ada, tn):
    depth, d, n = w_ada.shape
    rows = c_pad.shape[0]
    return pl.pallas_call(
        _ada_kernel,
        grid=(depth, n // tn),
        in_specs=[
            pl.BlockSpec((rows, d), lambda l, j: (0, 0)),
            pl.BlockSpec((1, d, tn), lambda l, j: (l, 0, j)),
            pl.BlockSpec((1, 1, tn), lambda l, j: (l, 0, j)),
        ],
        out_specs=pl.BlockSpec((1, rows, tn), lambda l, j: (l, 0, j)),
        out_shape=jax.ShapeDtypeStruct((depth, rows, n), F32),
        compiler_params=_cparams(("parallel", "parallel")),
        name="ada",
    )(c_pad, w_ada, b_ada.reshape(depth, 1, n))


def _in_proj_kernel(x_ref, g_ref, sh_ref, sc_ref, w_ref, wx_ref, *rest, n_scaled, scale, n_ride):
    ride_in = rest[:n_ride]
    o_ref, ox_ref = rest[n_ride:n_ride + 2]
    ride_out = rest[n_ride + 2:2 * n_ride + 2]
    u_ref = rest[-1]
    j = pl.program_id(1)

    @pl.when(j == 0)
    def _():
        u = _norm_mod(x_ref[...], g_ref[...], sh_ref[0], sc_ref[0]).astype(BF16)
        u_ref[...] = u
        ox_ref[...] = jnp.dot(u, wx_ref[0], preferred_element_type=F32)

    acc = jnp.dot(u_ref[...], w_ref[0], preferred_element_type=F32)
    if n_scaled:
        acc = acc * jnp.where(j < n_scaled, scale, 1.0)
    o_ref[0] = acc.astype(o_ref.dtype)
    for src, dst in zip(ride_in, ride_out):
        dst[...] = src[...].astype(BF16)


def _ride_rows(w, steps):
    layers, r, _ = w.shape
    for rb in range(16, r + 1, 16):
        if r % rb == 0 and layers * (r // rb) <= steps:
            return rb
    raise ValueError("weight too large to ride along")


def _in_proj(h, g, mod, sh_row, sc_row, w, w_layer, n, wx, seq, tm, tn, n_scaled=0, scale=1.0,
             ride_along=()):
    m, d = h.shape
    per_b = seq // tm
    n_j = n // tn
    steps = (m // tm) * n_j
    ride_specs, ride_shapes = [], []
    for wr in ride_along:
        rb = _ride_rows(wr, steps)
        per_layer = wr.shape[1] // rb
        last = wr.shape[0] * per_layer - 1

        def slab(i, j, per_layer=per_layer, last=last):
            s = jnp.minimum(i * n_j + j, last)
            return (s // per_layer, s % per_layer, 0)

        ride_specs.append(pl.BlockSpec((1, rb, wr.shape[2]), slab))
        ride_shapes.append(jax.ShapeDtypeStruct(wr.shape, BF16))
    return pl.pallas_call(
        functools.partial(_in_proj_kernel, n_scaled=n_scaled, scale=scale, n_ride=len(ride_along)),
        grid=(m // tm, n_j),
        in_specs=[
            pl.BlockSpec((tm, d), lambda i, j: (i, 0)),
            pl.BlockSpec((1, d), lambda i, j: (0, 0)),
            pl.BlockSpec((1, 1, d), lambda i, j: (sh_row + i // per_b, 0, 0)),
            pl.BlockSpec((1, 1, d), lambda i, j: (sc_row + i // per_b, 0, 0)),
            pl.BlockSpec((1, d, tn), lambda i, j: (w_layer, 0, j)),
            pl.BlockSpec((1, d, LANES), lambda i, j: (w_layer, 0, 0)),
        ] + ride_specs,
        out_specs=[
            pl.BlockSpec((1, tm, tn), lambda i, j: (j, i, 0)),
            pl.BlockSpec((tm, LANES), lambda i, j: (i, 0)),
        ] + ride_specs,
        out_shape=[
            jax.ShapeDtypeStruct((n_j, m, tn), BF16),
            jax.ShapeDtypeStruct((m, LANES), F32),
        ] + ride_shapes,
        scratch_shapes=[pltpu.VMEM((tm, d), BF16)],
        compiler_params=_cparams(("arbitrary", "arbitrary")),
        name="in_proj",
    )(h, g, mod, mod, w, wx, *ride_along)


def _out_proj_kernel(a_ref, w_ref, h_ref, gate_ref, o_ref):
    acc = jnp.dot(a_ref[...], w_ref[0], preferred_element_type=F32)
    o_ref[...] = h_ref[...] + gate_ref[0] * acc


def _out_proj(a, w, w_layer, h, mod, gate_row, seq, tm, tn):
    m, k = a.shape
    d = w.shape[2]
    per_b = seq // tm
    return pl.pallas_call(
        _out_proj_kernel,
        grid=(m // tm, d // tn),
        in_specs=[
            pl.BlockSpec((tm, k), lambda i, j: (i, 0)),
            pl.BlockSpec((1, k, tn), lambda i, j: (w_layer, 0, j)),
            pl.BlockSpec((tm, tn), lambda i, j: (i, j)),
            pl.BlockSpec((1, 1, tn), lambda i, j: (gate_row + i // per_b, 0, j)),
        ],
        out_specs=pl.BlockSpec((tm, tn), lambda i, j: (i, j)),
        out_shape=jax.ShapeDtypeStruct((m, d), F32),
        compiler_params=_cparams(("parallel", "arbitrary")),
        name="out_proj",
    )(a, w, h, mod)


def _mlp_kernel(h_ref, g_ref, sh_ref, sc_ref, gate_ref, wu_ref, wd_ref, fn_ref, o_ref, u_ref,
                *, final_norm):
    f = pl.program_id(1)

    @pl.when(f == 0)
    def _():
        u_ref[...] = _norm_mod(h_ref[...], g_ref[...], sh_ref[0], sc_ref[0]).astype(BF16)
        o_ref[...] = jnp.zeros_like(o_ref)

    a = jnp.dot(u_ref[...], wu_ref[0], preferred_element_type=F32)
    a = jnp.square(jnp.maximum(a, 0.0)).astype(BF16)
    o_ref[...] += jnp.dot(a, wd_ref[0], preferred_element_type=F32)

    @pl.when(f == pl.num_programs(1) - 1)
    def _():
        y = h_ref[...] + gate_ref[0] * o_ref[...]
        if final_norm:
            ms = jnp.mean(y * y, axis=-1, keepdims=True)
            y = y * lax.rsqrt(ms + EPS) * fn_ref[...]
        o_ref[...] = y


def _mlp(h, g, mod, sh_row, sc_row, gate_row, w_up, w_down, layer, fn, seq, tm, tf, final_norm):
    m, d = h.shape
    dff = w_up.shape[2]
    per_b = seq // tm
    return pl.pallas_call(
        functools.partial(_mlp_kernel, final_norm=final_norm),
        grid=(m // tm, dff // tf),
        in_specs=[
            pl.BlockSpec((tm, d), lambda i, f: (i, 0)),
            pl.BlockSpec((1, d), lambda i, f: (0, 0)),
            pl.BlockSpec((1, 1, d), lambda i, f: (sh_row + i // per_b, 0, 0)),
            pl.BlockSpec((1, 1, d), lambda i, f: (sc_row + i // per_b, 0, 0)),
            pl.BlockSpec((1, 1, d), lambda i, f: (gate_row + i // per_b, 0, 0)),
            pl.BlockSpec((1, d, tf), lambda i, f: (layer, 0, f)),
            pl.BlockSpec((1, tf, d), lambda i, f: (layer, f, 0)),
            pl.BlockSpec((1, d), lambda i, f: (0, 0)),
        ],
        out_specs=pl.BlockSpec((tm, d), lambda i, f: (i, 0)),
        out_shape=jax.ShapeDtypeStruct((m, d), F32),
        scratch_shapes=[pltpu.VMEM((tm, d), BF16)],
        compiler_params=_cparams(("parallel", "arbitrary")),
        name="mlp",
    )(h, g, mod, mod, mod, w_up, w_down, fn)


def _ssd_dt_kernel(dt_ref, dtb_ref, alog_ref, sc_ref, acum_ref, cd_ref):
    ck = SSD_CHUNK
    a_neg = -jnp.exp(alog_ref[...])
    for c in range(dt_ref.shape[0] // ck):
        rows = slice(c * ck, (c + 1) * ck)
        dt = _softplus(dt_ref[rows, :] + dtb_ref[...])
        acum = _cumsum_rows(dt * a_neg)
        a_last = acum[ck - 1:ck, :]
        sc_ref[0, rows, :] = dt
        sc_ref[1, rows, :] = dt * jnp.exp(a_last - acum)
        sc_ref[2, rows, :] = jnp.exp(acum)
        acum_ref[rows, :] = acum
        cd_ref[c] = jnp.broadcast_to(jnp.exp(a_last), (8, LANES))


def _ssd_dt(dt_raw, dtb, alog, blk):
    m = dt_raw.shape[0]
    per = blk // SSD_CHUNK
    return pl.pallas_call(
        _ssd_dt_kernel,
        grid=(m // blk,),
        in_specs=[
            pl.BlockSpec((blk, LANES), lambda i: (i, 0)),
            pl.BlockSpec((1, LANES), lambda i: (0, 0)),
            pl.BlockSpec((1, LANES), lambda i: (0, 0)),
        ],
        out_specs=[
            pl.BlockSpec((3, blk, LANES), lambda i: (0, i, 0)),
            pl.BlockSpec((blk, LANES), lambda i: (i, 0)),
            pl.BlockSpec((per, 8, LANES), lambda i: (i, 0, 0)),
        ],
        out_shape=[
            jax.ShapeDtypeStruct((3, m, LANES), F32),
            jax.ShapeDtypeStruct((m, LANES), F32),
            jax.ShapeDtypeStruct((m // SSD_CHUNK, 8, LANES), F32),
        ],
        compiler_params=_cparams(("parallel",)),
        name="ssd_dt",
    )(dt_raw, dtb, alog)


def _ssd_kernel(z_ref, x_ref, b_ref, c_ref, sc_ref, acum_ref, cd_ref, wx_ref, wb_ref, wc_ref,
                bx_ref, bb_ref, bc_ref, d_ref, gn_ref, o_ref, state_ref, xbc_ref, *, t_blk):
    g = pl.program_id(1)
    t = pl.program_id(2)
    ck = SSD_CHUNK
    gw = SSD_GROUP_WIDTH
    cw = gw + 2 * SSD_STATE

    @pl.when(t == 0)
    def _():
        state_ref[...] = jnp.zeros_like(state_ref)
        xbc_ref[0:ck, :] = jnp.zeros((ck, cw), BF16)

    xbc_ref[ck:ck + t_blk, 0:gw] = x_ref[0]
    xbc_ref[ck:ck + t_blk, gw:gw + SSD_STATE] = b_ref[0]
    xbc_ref[ck:ck + t_blk, gw + SSD_STATE:cw] = c_ref[0]
    conv_w = jnp.concatenate([wx_ref[...], wb_ref[...], wc_ref[...]], axis=1)
    conv_b = jnp.concatenate([bx_ref[...], bb_ref[...], bc_ref[...]], axis=1)

    sh_r = lax.broadcasted_iota(jnp.int32, ((SSD_CONV - 1) * ck, 2 * ck), 0)
    sh_c = lax.broadcasted_iota(jnp.int32, ((SSD_CONV - 1) * ck, 2 * ck), 1)
    shift = (sh_c == (sh_r & (ck - 1)) + (ck - (SSD_CONV - 1)) + (sh_r >> 7)).astype(BF16)
    assert ck == 128

    row = lax.broadcasted_iota(jnp.int32, (ck, ck), 0)
    col = lax.broadcasted_iota(jnp.int32, (ck, ck), 1)
    causal = row >= col
    low_half = col < SSD_HEAD_DIM
    e_h = lax.broadcasted_iota(jnp.int32, (LANES, gw), 0)
    e_c = lax.broadcasted_iota(jnp.int32, (LANES, gw), 1)
    expand = (e_h == g * SSD_HEADS_PER_GROUP + (e_c >> 6)).astype(BF16)
    expand2 = jnp.concatenate([expand, expand], axis=0)
    expand3 = jnp.concatenate([expand2, expand], axis=0)
    to_lane0 = (LANES - g * SSD_HEADS_PER_GROUP) % LANES

    nt = (((1,), (1,)), ((), ()))
    tn = (((0,), (0,)), ((), ()))

    def chunk(c, carry):
        r0 = pl.multiple_of(c * ck, ck)
        window = xbc_ref[pl.ds(r0, 2 * ck), :]
        back = jnp.dot(shift, window, preferred_element_type=F32)
        acc = conv_b + conv_w[SSD_CONV - 1:SSD_CONV, :] * window[ck:2 * ck].astype(F32)
        for k in range(SSD_CONV - 1):
            acc = acc + conv_w[k:k + 1, :] * back[k * ck:(k + 1) * ck]
        conv = _silu(acc)
        xs = conv[:, 0:gw]
        bm = conv[:, gw:gw + SSD_STATE].astype(BF16)
        cm = conv[:, gw + SSD_STATE:cw].astype(BF16)

        scales = jnp.concatenate([sc_ref[i, pl.ds(r0, ck), :] for i in range(3)], axis=0)
        acum = acum_ref[pl.ds(r0, ck), :]
        ex = jnp.dot(jnp.concatenate(_split3(scales)[:2], axis=1), expand2,
                     preferred_element_type=F32)
        dt_x = ex[0:ck]
        wend_x = ex[ck:2 * ck]
        ea_x = ex[2 * ck:3 * ck]
        cd_x = jnp.dot(jnp.concatenate(_split3(cd_ref[c]), axis=1), expand3,
                       preferred_element_type=F32)[0:1]
        acum_g = pltpu.roll(acum, to_lane0, 1)
        acum_t = jnp.transpose(acum_g)

        scores = lax.dot_general(cm, bm, nt, preferred_element_type=F32)
        xdt = (xs * dt_x).astype(BF16)
        y_pairs = []
        for p in range(SSD_HEADS_PER_GROUP // 2):
            x_pair = xdt[:, p * LANES:(p + 1) * LANES]
            rhs = jnp.concatenate([jnp.where(low_half, x_pair, jnp.zeros_like(x_pair)),
                                   jnp.where(low_half, jnp.zeros_like(x_pair), x_pair)], axis=0)
            lhs = []
            for j in (2 * p, 2 * p + 1):
                a_col = jnp.broadcast_to(acum_g[:, j:j + 1], (ck, ck))
                seg = a_col - acum_t[j:j + 1, :]
                decay = jnp.exp(jnp.where(causal, seg, -jnp.inf))
                lhs.append((scores * decay).astype(BF16))
            y_pairs.append(jnp.dot(jnp.concatenate(lhs, axis=1), rhs, preferred_element_type=F32))
        y = jnp.concatenate(y_pairs, axis=1)

        state = state_ref[...]
        y = y + jnp.dot(cm, state.astype(BF16), preferred_element_type=F32) * ea_x
        state_ref[...] = state * cd_x + lax.dot_general(
            bm, (xs * wend_x).astype(BF16), tn, preferred_element_type=F32)

        y = y + xs * d_ref[...]
        y = y * _silu(z_ref[0, pl.ds(r0, ck), :].astype(F32))
        ms = jnp.mean(y * y, axis=-1, keepdims=True)
        y = y * lax.rsqrt(ms + EPS) * gn_ref[...]
        o_ref[pl.ds(r0, ck), :] = y.astype(o_ref.dtype)
        return carry

    lax.fori_loop(0, t_blk // ck, chunk, 0, unroll=4)
    xbc_ref[0:ck, :] = xbc_ref[t_blk:t_blk + ck, :]


def _ssd(zxbc, scales, acum, chunk_decay, conv_w, conv_b, d_x, gnorm, batch, seq, t_blk, tn_in):
    m = acum.shape[0]
    gw = SSD_GROUP_WIDTH
    d_inner = gnorm.shape[1]
    groups = d_inner // gw
    n_t = seq // t_blk
    per_tile = tn_in // gw
    x_tile0 = d_inner // tn_in
    b_tile = 2 * d_inner // tn_in
    c_tile = b_tile + 1
    assert groups * SSD_STATE == tn_in and tn_in % gw == 0
    xw0 = 0
    bw0 = d_inner // SSD_STATE
    cw0 = bw0 + groups

    def rows(b, t):
        return b * n_t + t

    return pl.pallas_call(
        functools.partial(_ssd_kernel, t_blk=t_blk),
        grid=(batch, groups, n_t),
        in_specs=[
            pl.BlockSpec((1, t_blk, gw), lambda b, g, t: (g // per_tile, rows(b, t), g % per_tile)),
            pl.BlockSpec((1, t_blk, gw), lambda b, g, t: (x_tile0 + g // per_tile, rows(b, t), g % per_tile)),
            pl.BlockSpec((1, t_blk, SSD_STATE), lambda b, g, t: (b_tile, rows(b, t), g)),
            pl.BlockSpec((1, t_blk, SSD_STATE), lambda b, g, t: (c_tile, rows(b, t), g)),
            pl.BlockSpec((3, t_blk, LANES), lambda b, g, t: (0, rows(b, t), 0)),
            pl.BlockSpec((t_blk, LANES), lambda b, g, t: (rows(b, t), 0)),
            pl.BlockSpec((t_blk // SSD_CHUNK, 8, LANES), lambda b, g, t: (rows(b, t), 0, 0)),
            pl.BlockSpec((SSD_CONV, gw), lambda b, g, t: (0, xw0 + g)),
            pl.BlockSpec((SSD_CONV, SSD_STATE), lambda b, g, t: (0, bw0 + g)),
            pl.BlockSpec((SSD_CONV, SSD_STATE), lambda b, g, t: (0, cw0 + g)),
            pl.BlockSpec((1, gw), lambda b, g, t: (0, xw0 + g)),
            pl.BlockSpec((1, SSD_STATE), lambda b, g, t: (0, bw0 + g)),
            pl.BlockSpec((1, SSD_STATE), lambda b, g, t: (0, cw0 + g)),
            pl.BlockSpec((1, gw), lambda b, g, t: (0, g)),
            pl.BlockSpec((1, gw), lambda b, g, t: (0, g)),
        ],
        out_specs=pl.BlockSpec((t_blk, gw), lambda b, g, t: (rows(b, t), g)),
        out_shape=jax.ShapeDtypeStruct((m, d_inner), BF16),
        scratch_shapes=[
            pltpu.VMEM((SSD_STATE, gw), F32),
            pltpu.VMEM((t_blk + SSD_CHUNK, gw + 2 * SSD_STATE), BF16),
        ],
        compiler_params=_cparams(("parallel", "parallel", "arbitrary")),
        name="ssd",
    )(zxbc, zxbc, zxbc, zxbc, scales, acum, chunk_decay, conv_w, conv_w, conv_w, conv_b, conv_b, conv_b,
      d_x, gnorm)


def _fcum_kernel(f_ref, b_ref, o_ref, ot_ref):
    x = f_ref[...] + b_ref[...]
    cum = _cumsum_rows(-_softplus(-x)) * LOG2E
    o_ref[0] = cum
    cum_t = jnp.transpose(cum)
    tq = ot_ref.shape[3]
    for i in range(ot_ref.shape[2]):
        ot_ref[0, :, i, :] = cum_t[:, i * tq:(i + 1) * tq]


def _fcum(f_logit, b_f, batch, seq, tq):
    return pl.pallas_call(
        _fcum_kernel,
        grid=(batch,),
        in_specs=[
            pl.BlockSpec((seq, LANES), lambda b: (b, 0)),
            pl.BlockSpec((1, LANES), lambda b: (0, 0)),
        ],
        out_specs=[
            pl.BlockSpec((1, seq, LANES), lambda b: (b, 0, 0)),
            pl.BlockSpec((1, LANES, seq // tq, tq), lambda b: (b, 0, 0, 0)),
        ],
        out_shape=[
            jax.ShapeDtypeStruct((batch, seq, LANES), F32),
            jax.ShapeDtypeStruct((batch, LANES, seq // tq, tq), F32),
        ],
        compiler_params=_cparams(("parallel",)),
        name="fcum",
    )(f_logit, b_f)


def _key_bias_columns(cum, head):
    r = lax.broadcasted_iota(jnp.int32, (LANES, LANES), 0)
    c = lax.broadcasted_iota(jnp.int32, (LANES, LANES), 1)
    out = None
    for i, piece in enumerate(_split3(cum)):
        sel = ((r == head) & (c == i)).astype(BF16)
        term = jnp.dot(piece, sel, preferred_element_type=F32)
        out = term if out is None else out + term
    return (-out).astype(BF16)


def _attn_kernel(q_ref, k_ref, v_ref, cq_ref, ck_ref, o_ref, kx_ref, vt_ref, s_ref, p_ref, acc_ref,
                 *, tq, tk, seq):
    head = pl.program_id(1)
    hd = FOX_HEAD_DIM
    nt = (((1,), (1,)), ((), ()))
    assert tq == 2 * tk

    for rc in range(seq // tk):
        r0 = rc * tk
        kx_ref[rc, :, 0:hd] = k_ref[0, r0:r0 + tk, :]
        kx_ref[rc, :, hd:2 * hd] = _key_bias_columns(ck_ref[0, r0:r0 + tk, :], head)
        vt_ref[rc] = jnp.transpose(v_ref[0, r0:r0 + tk, :].astype(F32)).astype(BF16)

    lane = lax.broadcasted_iota(jnp.int32, (tq, hd), 1)

    def queries(qb):
        rows = q_ref[0, pl.ds(pl.multiple_of(qb * tq, tq), tq), :]
        return jnp.concatenate([rows, (lane < 3).astype(BF16)], axis=1)

    def first_scores(qb):
        s_ref[0] = lax.dot_general(kx_ref[0], queries(qb), nt, preferred_element_type=F32)

    first_scores(0)

    def qblock(qi, carry_q):
        q0 = pl.multiple_of(qi * tq, tq)
        qx = queries(qi)
        cq = cq_ref[0, head % 8, pl.ds(qi, 1), :]

        def s1(t, slot):
            s_ref[slot] = lax.dot_general(kx_ref[t], qx, nt, preferred_element_type=F32)

        def s2(slot, m, l, key_offset=None):
            st = s_ref[slot]
            if key_offset is not None:
                k_idx = lax.broadcasted_iota(jnp.int32, st.shape, 0) + key_offset
                q_idx = lax.broadcasted_iota(jnp.int32, st.shape, 1)
                st = jnp.where(k_idx <= q_idx, st, -jnp.inf)
            m_new = jnp.maximum(m, jnp.max(st, axis=0, keepdims=True) + cq)
            alpha = jnp.exp2(m - m_new)
            p = jnp.exp2(st - (m_new - cq))
            p_ref[slot] = p.astype(BF16)
            return m_new, alpha * l + jnp.sum(p, axis=0, keepdims=True), alpha

        def s3(t, slot, alpha):
            acc_ref[...] = alpha * acc_ref[...] + jnp.dot(vt_ref[t], p_ref[slot],
                                                          preferred_element_type=F32)

        acc_ref[...] = jnp.zeros_like(acc_ref)
        p_ref[1] = jnp.zeros(p_ref.shape[1:], BF16)

        def body(j, carry):
            m, l, alpha = carry
            t = 2 * j
            s3(jnp.maximum(t - 1, 0), 1, alpha)
            s1(t + 1, 1)
            m, l, alpha = s2(0, m, l)
            s3(t, 0, alpha)
            s1(t + 2, 0)
            return s2(1, m, l)

        m0 = jnp.full((1, tq), -jnp.inf, F32)
        l0 = jnp.zeros((1, tq), F32)
        m, l, alpha = lax.fori_loop(0, qi, body, (m0, l0, jnp.ones((1, tq), F32)))
        t = 2 * qi
        s3(jnp.maximum(t - 1, 0), 1, alpha)
        st = lax.dot_general(kx_ref[t + 1], qx[tk:tq], nt, preferred_element_type=F32)
        m, l, alpha = s2(0, m, l, key_offset=0)
        s3(t, 0, alpha)
        k_idx = lax.broadcasted_iota(jnp.int32, st.shape, 0)
        q_idx = lax.broadcasted_iota(jnp.int32, st.shape, 1)
        st = jnp.where(k_idx <= q_idx, st, -jnp.inf)
        m_hi, l_hi, cq_hi = m[:, tk:tq], l[:, tk:tq], cq[:, tk:tq]
        m_new = jnp.maximum(m_hi, jnp.max(st, axis=0, keepdims=True) + cq_hi)
        alpha_hi = jnp.exp2(m_hi - m_new)
        p = jnp.exp2(st - (m_new - cq_hi))
        l_hi = alpha_hi * l_hi + jnp.sum(p, axis=0, keepdims=True)
        acc_ref[:, tk:tq] = alpha_hi * acc_ref[:, tk:tq] + jnp.dot(
            vt_ref[t + 1], p.astype(BF16), preferred_element_type=F32)
        l = jnp.concatenate([l[:, 0:tk], l_hi], axis=1)
        first_scores(jnp.minimum(qi + 1, seq // tq - 1))
        o_ref[pl.ds(q0, tq), :] = jnp.transpose(acc_ref[...] / l).astype(o_ref.dtype)
        return carry_q

    lax.fori_loop(0, seq // tq, qblock, 0)


def _attn(qkv, cum, cum_t, batch, seq, heads, tq, tn_in):
    m = qkv.shape[1]
    hd = FOX_HEAD_DIM
    per_tile = tn_in // hd
    tiles = heads // per_tile
    n_q = seq // tq
    tk = tq // 2
    return pl.pallas_call(
        functools.partial(_attn_kernel, tq=tq, tk=tk, seq=seq),
        grid=(batch, heads),
        in_specs=[
            pl.BlockSpec((1, seq, hd), lambda b, h: (h // per_tile, b, h % per_tile)),
            pl.BlockSpec((1, seq, hd), lambda b, h: (tiles + h // per_tile, b, h % per_tile)),
            pl.BlockSpec((1, seq, hd), lambda b, h: (2 * tiles + h // per_tile, b, h % per_tile)),
            pl.BlockSpec((1, 8, n_q, tq), lambda b, h: (b, h // 8, 0, 0)),
            pl.BlockSpec((1, seq, LANES), lambda b, h: (b, 0, 0)),
        ],
        out_specs=pl.BlockSpec((seq, hd), lambda b, h: (b, h)),
        out_shape=jax.ShapeDtypeStruct((m, heads * hd), BF16),
        scratch_shapes=[
            pltpu.VMEM((seq // tk, tk, 2 * hd), BF16),
            pltpu.VMEM((seq // tk, hd, tk), BF16),
            pltpu.VMEM((2, tk, tq), F32),
            pltpu.VMEM((2, tk, tq), BF16),
            pltpu.VMEM((hd, tq), F32),
        ],
        compiler_params=_cparams(("parallel", "parallel")),
        name="attn",
    )(qkv, qkv, qkv, cum_t, cum)


def _pad_cols(a, n):
    return jnp.pad(a, ((0, 0), (0, n - a.shape[1])))


def kernel(x, c, norm_mix, norm_mlp, w_ada, b_ada, w_up, w_down, ssd_w_in, ssd_conv_w, ssd_conv_b,
           ssd_dt_bias, ssd_A_log, ssd_D, ssd_gnorm, ssd_w_out, fox_w_in, fox_b_f, fox_w_out,
           final_norm):
    batch, seq, d = x.shape
    m = batch * seq
    depth = w_ada.shape[0]
    d_inner = ssd_gnorm.shape[1]
    ssd_heads = ssd_dt_bias.shape[1]
    fox_heads = fox_b_f.shape[1]
    fox_width = fox_heads * FOX_HEAD_DIM
    assert d_inner == ssd_heads * SSD_HEAD_DIM and ssd_heads <= LANES and fox_heads <= LANES

    tm = min(1024, seq)
    tn_in = (d_inner // SSD_GROUP_WIDTH) * SSD_STATE
    tn_out = min(512, d)
    tf = min(512, w_up.shape[2])
    t_blk = min(1024, seq)
    tq = min(512, seq)
    assert fox_width % tn_in == 0 and d_inner % tn_in == 0

    rows = 8
    c_pad = jnp.pad(c, ((0, rows - batch), (0, 0)))
    mod = _ada(c_pad, w_ada, b_ada, tn=min(1024, N_MOD * d))[:, :batch]
    mod = mod.reshape(depth, batch, N_MOD, d).transpose(0, 2, 1, 3).reshape(depth * N_MOD * batch, 1, d)

    def mod_row(layer, which):
        return (layer * N_MOD + which) * batch

    plain = {"w_up": w_up, "w_down": w_down, "ssd_w_out": ssd_w_out, "fox_w_out": fox_w_out}
    h = x.reshape(m, d)
    for layer in range(depth):
        j = layer // 2
        g_mix = norm_mix[layer].reshape(1, d)
        g_mlp = norm_mlp[layer].reshape(1, d)
        if layer % 2 == 0:
            n_main = 2 * d_inner + 2 * (d_inner // SSD_GROUP_WIDTH) * SSD_STATE
            w_main, w_side = _cast_weights_t(jnp.transpose(ssd_w_in, (0, 2, 1)), n_main, tn_in)
            ride = tuple(plain.values()) if layer == 0 else ()
            zxbc, dt_raw, *cast = _in_proj(h, g_mix, mod, mod_row(layer, 0), mod_row(layer, 1),
                                           w_main, j, n_main, w_side, seq, tm, tn_in, ride_along=ride)
            if cast:
                plain = dict(zip(plain, cast))
            scales, acum, chunk_decay = _ssd_dt(dt_raw, _pad_cols(ssd_dt_bias[j].reshape(1, -1), LANES),
                                                _pad_cols(ssd_A_log[j].reshape(1, -1), LANES), t_blk)
            y = _ssd(zxbc, scales, acum, chunk_decay, ssd_conv_w[j], ssd_conv_b[j].reshape(1, -1),
                     jnp.repeat(ssd_D[j], SSD_HEAD_DIM).reshape(1, d_inner),
                     ssd_gnorm[j].reshape(1, d_inner), batch, seq, t_blk, tn_in)
            h = _out_proj(y, plain["ssd_w_out"], j, h, mod, mod_row(layer, 2), seq, tm, tn_out)
        else:
            w_main, w_side = _cast_weights_t(jnp.transpose(fox_w_in, (0, 2, 1)), 3 * fox_width, tn_in)
            qkv, f_logit = _in_proj(h, g_mix, mod, mod_row(layer, 0), mod_row(layer, 1),
                                    w_main, j, 3 * fox_width, w_side, seq, tm, tn_in,
                                    n_scaled=fox_width // tn_in, scale=FOX_HEAD_DIM ** -0.5 * LOG2E)
            cum, cum_t = _fcum(f_logit, _pad_cols(fox_b_f[j].reshape(1, -1), LANES), batch, seq, tq)
            o = _attn(qkv, cum, cum_t, batch, seq, fox_heads, tq, tn_in)
            h = _out_proj(o, plain["fox_w_out"], j, h, mod, mod_row(layer, 2), seq, tm, min(2 * tn_out, d))
        last = layer == depth - 1
        h = _mlp(h, g_mlp, mod, mod_row(layer, 3), mod_row(layer, 4), mod_row(layer, 5),
                 plain["w_up"], plain["w_down"], layer, final_norm.reshape(1, d),
                 seq, tm, tf, final_norm=last)
    return h.reshape(batch, seq, d)
```

```python
import functools

import jax
import jax.numpy as jnp
from jax import lax
from jax.experimental import pallas as pl
from jax.experimental.pallas import tpu as pltpu

F32 = jnp.float32
BF16 = jnp.bfloat16

EPS = 1e-6
N_MOD = 6
LANES = 128
SSD_HEAD_DIM = 64
SSD_STATE = 128
SSD_CONV = 4
SSD_HEADS_PER_GROUP = 8
SSD_GROUP_WIDTH = SSD_HEADS_PER_GROUP * SSD_HEAD_DIM
SSD_CHUNK = 128
FOX_HEAD_DIM = 128
LOG2E = 1.4426950408889634
VMEM_LIMIT = 56 * 1024 * 1024


def _cparams(sem):
    return pltpu.CompilerParams(dimension_semantics=sem, vmem_limit_bytes=VMEM_LIMIT)


def _silu(x):
    half = 0.5 * x
    return half + half * jnp.tanh(half)


def _softplus(x):
    return jnp.maximum(x, 0.0) + jnp.log1p(jnp.exp(-jnp.abs(x)))


def _split3(x):
    hi = x.astype(BF16)
    r1 = x - hi.astype(F32)
    mid = r1.astype(BF16)
    lo = (r1 - mid.astype(F32)).astype(BF16)
    return hi, mid, lo


def _cumsum_rows(x):
    n = x.shape[0]
    row = lax.broadcasted_iota(jnp.int32, x.shape, 0)
    k = 1
    while k < n:
        x = x + jnp.where(row >= k, pltpu.roll(x, k, 0), 0.0)
        k *= 2
    return x


def _norm_mod(x, g, shift, scale):
    ms = jnp.mean(x * x, axis=-1, keepdims=True)
    return x * lax.rsqrt(ms + EPS) * (g * (1.0 + scale)) + shift


def _cast_t_kernel(w_ref, o_ref):
    w = w_ref[0]
    pad = o_ref.shape[2] - w.shape[0]
    if pad:
        w = jnp.concatenate([w, jnp.zeros((pad, w.shape[1]), F32)], axis=0)
    o_ref[0] = jnp.transpose(w).astype(BF16)


def _cast_t_main(w_t, c_main, cols_blk):
    layers, c, k = w_t.shape
    assert c_main % cols_blk == 0
    return pl.pallas_call(
        _cast_t_kernel,
        grid=(layers, c_main // cols_blk),
        in_specs=[pl.BlockSpec((1, cols_blk, k), lambda l, i: (l, i, 0))],
        out_specs=pl.BlockSpec((1, k, cols_blk), lambda l, i: (l, 0, i)),
        out_shape=jax.ShapeDtypeStruct((layers, k, c_main), BF16),
        compiler_params=_cparams(("parallel", "parallel")),
        name="cast_t",
    )(w_t)


def _cast_t_side(w_t, c_main):
    layers, c, k = w_t.shape
    side = c - c_main
    assert 0 < side <= LANES and c_main % side == 0 and side % 8 == 0
    return pl.pallas_call(
        _cast_t_kernel,
        grid=(layers,),
        in_specs=[pl.BlockSpec((1, side, k), lambda l: (l, c_main // side, 0))],
        out_specs=pl.BlockSpec((1, k, LANES), lambda l: (l, 0, 0)),
        out_shape=jax.ShapeDtypeStruct((layers, k, LANES), BF16),
        compiler_params=_cparams(("parallel",)),
        name="cast_t_side",
    )(w_t)


def _ada_kernel(c_ref, w_ref, b_ref, o_ref):
    cond = _silu(c_ref[...])
    o_ref[0] = jnp.dot(cond.astype(BF16), w_ref[0].astype(BF16),
                       preferred_element_type=F32) + b_ref[0]


def _ada(c_pad, w_ada, b_ada, tn):
    depth, d, n = w_ada.shape
    rows = c_pad.shape[0]
    return pl.pallas_call(
        _ada_kernel,
        grid=(depth, n // tn),
        in_specs=[
            pl.BlockSpec((rows, d), lambda l, j: (0, 0)),
            pl.BlockSpec((1, d, tn), lambda l, j: (l, 0, j)),
            pl.BlockSpec((1, 1, tn), lambda l, j: (l, 0, j)),
        ],
        out_specs=pl.BlockSpec((1, rows, tn), lambda l, j: (l, 0, j)),
        out_shape=jax.ShapeDtypeStruct((depth, rows, n), F32),
        compiler_params=_cparams(("parallel", "parallel")),
        name="ada",
    )(c_pad, w_ada, b_ada.reshape(depth, 1, n))


def _in_proj_kernel(x_ref, g_ref, sh_ref, sc_ref, w_ref, wx_ref, *rest, n_scaled, scale, n_ride, n_ride_t):
    n_all = n_ride + n_ride_t
    ride_in = rest[:n_all]
    o_ref, ox_ref = rest[n_all:n_all + 2]
    ride_out = rest[n_all + 2:2 * n_all + 2]
    u_ref = rest[-1]
    j = pl.program_id(1)

    @pl.when(j == 0)
    def _():
        u = _norm_mod(x_ref[...], g_ref[...], sh_ref[0], sc_ref[0]).astype(BF16)
        u_ref[...] = u
        ox_ref[...] = jnp.dot(u, wx_ref[0], preferred_element_type=F32)

    acc = jnp.dot(u_ref[...], w_ref[0], preferred_element_type=F32)
    if n_scaled:
        acc = acc * jnp.where(j < n_scaled, scale, 1.0)
    o_ref[0] = acc.astype(o_ref.dtype)
    for src, dst in zip(ride_in[:n_ride], ride_out[:n_ride]):
        dst[...] = src[...].astype(BF16)
    for src, dst in zip(ride_in[n_ride:], ride_out[n_ride:]):
        dst[0] = jnp.transpose(src[0]).astype(BF16)


def _ride_rows(w, steps):
    layers, r, _ = w.shape
    for rb in range(16, r + 1, 16):
        if r % rb == 0 and layers * (r // rb) <= steps:
            return rb
    raise ValueError("weight too large to ride along")


def _in_proj(h, g, mod, sh_row, sc_row, w, w_layer, n, wx, seq, tm, tn, n_scaled=0, scale=1.0,
             ride_along=(), ride_along_t=()):
    m, d = h.shape
    per_b = seq // tm
    n_j = n // tn
    steps = (m // tm) * n_j
    ride_specs, ride_shapes = [], []
    for wr in ride_along:
        rb = _ride_rows(wr, steps)
        per_layer = wr.shape[1] // rb
        last = wr.shape[0] * per_layer - 1

        def slab(i, j, per_layer=per_layer, last=last):
            s = jnp.minimum(i * n_j + j, last)
            return (s // per_layer, s % per_layer, 0)

        ride_specs.append(pl.BlockSpec((1, rb, wr.shape[2]), slab))
        ride_shapes.append(jax.ShapeDtypeStruct(wr.shape, BF16))
    ride_in_specs, ride_out_specs = list(ride_specs), list(ride_specs)
    for wt, c_main in ride_along_t:
        per_layer = c_main // LANES
        last = wt.shape[0] * per_layer - 1
        assert last < steps

        def slab_t(i, j, per_layer=per_layer, last=last):
            s = jnp.minimum(i * n_j + j, last)
            return s // per_layer, s % per_layer

        ride_in_specs.append(pl.BlockSpec((1, LANES, wt.shape[2]), lambda i, j, f=slab_t: (f(i, j)[0], f(i, j)[1], 0)))
        ride_out_specs.append(pl.BlockSpec((1, wt.shape[2], LANES), lambda i, j, f=slab_t: (f(i, j)[0], 0, f(i, j)[1])))
        ride_shapes.append(jax.ShapeDtypeStruct((wt.shape[0], wt.shape[2], c_main), BF16))
    return pl.pallas_call(
        functools.partial(_in_proj_kernel, n_scaled=n_scaled, scale=scale, n_ride=len(ride_along),
                          n_ride_t=len(ride_along_t)),
        grid=(m // tm, n_j),
        in_specs=[
            pl.BlockSpec((tm, d), lambda i, j: (i, 0)),
            pl.BlockSpec((1, d), lambda i, j: (0, 0)),
            pl.BlockSpec((1, 1, d), lambda i, j: (sh_row + i // per_b, 0, 0)),
            pl.BlockSpec((1, 1, d), lambda i, j: (sc_row + i // per_b, 0, 0)),
            pl.BlockSpec((1, d, tn), lambda i, j: (w_layer, 0, j)),
            pl.BlockSpec((1, d, LANES), lambda i, j: (w_layer, 0, 0)),
        ] + ride_in_specs,
        out_specs=[
            pl.BlockSpec((1, tm, tn), lambda i, j: (j, i, 0)),
            pl.BlockSpec((tm, LANES), lambda i, j: (i, 0)),
        ] + ride_out_specs,
        out_shape=[
            jax.ShapeDtypeStruct((n_j, m, tn), BF16),
            jax.ShapeDtypeStruct((m, LANES), F32),
        ] + ride_shapes,
        scratch_shapes=[pltpu.VMEM((tm, d), BF16)],
        compiler_params=_cparams(("arbitrary", "arbitrary")),
        name="in_proj",
    )(h, g, mod, mod, w, wx, *ride_along, *[wt for wt, _ in ride_along_t])


def _out_proj_kernel(a_ref, w_ref, h_ref, gate_ref, o_ref):
    acc = jnp.dot(a_ref[...], w_ref[0], preferred_element_type=F32)
    o_ref[...] = h_ref[...] + gate_ref[0] * acc


def _out_proj(a, w, w_layer, h, mod, gate_row, seq, tm, tn):
    m, k = a.shape
    d = w.shape[2]
    per_b = seq // tm
    return pl.pallas_call(
        _out_proj_kernel,
        grid=(m // tm, d // tn),
        in_specs=[
            pl.BlockSpec((tm, k), lambda i, j: (i, 0)),
            pl.BlockSpec((1, k, tn), lambda i, j: (w_layer, 0, j)),
            pl.BlockSpec((tm, tn), lambda i, j: (i, j)),
            pl.BlockSpec((1, 1, tn), lambda i, j: (gate_row + i // per_b, 0, j)),
        ],
        out_specs=pl.BlockSpec((tm, tn), lambda i, j: (i, j)),
        out_shape=jax.ShapeDtypeStruct((m, d), F32),
        compiler_params=_cparams(("parallel", "arbitrary")),
        name="out_proj",
    )(a, w, h, mod)


def _mlp_kernel(h_ref, g_ref, sh_ref, sc_ref, gate_ref, wu_ref, wd_ref, fn_ref, o_ref, u_ref,
                *, final_norm):
    f = pl.program_id(1)

    @pl.when(f == 0)
    def _():
        u_ref[...] = _norm_mod(h_ref[...], g_ref[...], sh_ref[0], sc_ref[0]).astype(BF16)
        o_ref[...] = jnp.zeros_like(o_ref)

    a = jnp.dot(u_ref[...], wu_ref[0], preferred_element_type=F32)
    a = jnp.square(jnp.maximum(a, 0.0)).astype(BF16)
    o_ref[...] += jnp.dot(a, wd_ref[0], preferred_element_type=F32)

    @pl.when(f == pl.num_programs(1) - 1)
    def _():
        y = h_ref[...] + gate_ref[0] * o_ref[...]
        if final_norm:
            ms = jnp.mean(y * y, axis=-1, keepdims=True)
            y = y * lax.rsqrt(ms + EPS) * fn_ref[...]
        o_ref[...] = y


def _mlp(h, g, mod, sh_row, sc_row, gate_row, w_up, w_down, layer, fn, seq, tm, tf, final_norm):
    m, d = h.shape
    dff = w_up.shape[2]
    per_b = seq // tm
    return pl.pallas_call(
        functools.partial(_mlp_kernel, final_norm=final_norm),
        grid=(m // tm, dff // tf),
        in_specs=[
            pl.BlockSpec((tm, d), lambda i, f: (i, 0)),
            pl.BlockSpec((1, d), lambda i, f: (0, 0)),
            pl.BlockSpec((1, 1, d), lambda i, f: (sh_row + i // per_b, 0, 0)),
            pl.BlockSpec((1, 1, d), lambda i, f: (sc_row + i // per_b, 0, 0)),
            pl.BlockSpec((1, 1, d), lambda i, f: (gate_row + i // per_b, 0, 0)),
            pl.BlockSpec((1, d, tf), lambda i, f: (layer, 0, f)),
            pl.BlockSpec((1, tf, d), lambda i, f: (layer, f, 0)),
            pl.BlockSpec((1, d), lambda i, f: (0, 0)),
        ],
        out_specs=pl.BlockSpec((tm, d), lambda i, f: (i, 0)),
        out_shape=jax.ShapeDtypeStruct((m, d), F32),
        scratch_shapes=[pltpu.VMEM((tm, d), BF16)],
        compiler_params=_cparams(("parallel", "arbitrary")),
        name="mlp",
    )(h, g, mod, mod, mod, w_up, w_down, fn)


def _ssd_dt_kernel(dt_ref, dtb_ref, alog_ref, sc_ref, acum_ref, cd_ref):
    ck = SSD_CHUNK
    a_neg = -jnp.exp(alog_ref[...])
    for c in range(dt_ref.shape[0] // ck):
        rows = slice(c * ck, (c + 1) * ck)
        dt = _softplus(dt_ref[rows, :] + dtb_ref[...])
        acum = _cumsum_rows(dt * a_neg)
        a_last = acum[ck - 1:ck, :]
        sc_ref[0, rows, :] = dt
        sc_ref[1, rows, :] = dt * jnp.exp(a_last - acum)
        sc_ref[2, rows, :] = jnp.exp(acum)
        acum_ref[rows, :] = acum
        cd_ref[c] = jnp.broadcast_to(jnp.exp(a_last), (8, LANES))


def _ssd_dt(dt_raw, dtb, alog, blk):
    m = dt_raw.shape[0]
    per = blk // SSD_CHUNK
    return pl.pallas_call(
        _ssd_dt_kernel,
        grid=(m // blk,),
        in_specs=[
            pl.BlockSpec((blk, LANES), lambda i: (i, 0)),
            pl.BlockSpec((1, LANES), lambda i: (0, 0)),
            pl.BlockSpec((1, LANES), lambda i: (0, 0)),
        ],
        out_specs=[
            pl.BlockSpec((3, blk, LANES), lambda i: (0, i, 0)),
            pl.BlockSpec((blk, LANES), lambda i: (i, 0)),
            pl.BlockSpec((per, 8, LANES), lambda i: (i, 0, 0)),
        ],
        out_shape=[
            jax.ShapeDtypeStruct((3, m, LANES), F32),
            jax.ShapeDtypeStruct((m, LANES), F32),
            jax.ShapeDtypeStruct((m // SSD_CHUNK, 8, LANES), F32),
        ],
        compiler_params=_cparams(("parallel",)),
        name="ssd_dt",
    )(dt_raw, dtb, alog)


def _ssd_kernel(z_ref, x_ref, b_ref, c_ref, sc_ref, acum_ref, cd_ref, wx_ref, wb_ref, wc_ref,
                bx_ref, bb_ref, bc_ref, d_ref, gn_ref, o_ref, state_ref, xbc_ref, *, t_blk):
    g = pl.program_id(1)
    t = pl.program_id(2)
    ck = SSD_CHUNK
    gw = SSD_GROUP_WIDTH
    cw = gw + 2 * SSD_STATE

    @pl.when(t == 0)
    def _():
        state_ref[...] = jnp.zeros_like(state_ref)
        xbc_ref[0:ck, :] = jnp.zeros((ck, cw), BF16)

    xbc_ref[ck:ck + t_blk, 0:gw] = x_ref[0]
    xbc_ref[ck:ck + t_blk, gw:gw + SSD_STATE] = b_ref[0]
    xbc_ref[ck:ck + t_blk, gw + SSD_STATE:cw] = c_ref[0]
    conv_w = jnp.concatenate([wx_ref[...], wb_ref[...], wc_ref[...]], axis=1)
    conv_b = jnp.concatenate([bx_ref[...], bb_ref[...], bc_ref[...]], axis=1)

    sh_r = lax.broadcasted_iota(jnp.int32, ((SSD_CONV - 1) * ck, 2 * ck), 0)
    sh_c = lax.broadcasted_iota(jnp.int32, ((SSD_CONV - 1) * ck, 2 * ck), 1)
    shift = (sh_c == (sh_r & (ck - 1)) + (ck - (SSD_CONV - 1)) + (sh_r >> 7)).astype(BF16)
    assert ck == 128

    row = lax.broadcasted_iota(jnp.int32, (ck, ck), 0)
    col = lax.broadcasted_iota(jnp.int32, (ck, ck), 1)
    causal = row >= col
    low_half = col < SSD_HEAD_DIM
    e_h = lax.broadcasted_iota(jnp.int32, (LANES, gw), 0)
    e_c = lax.broadcasted_iota(jnp.int32, (LANES, gw), 1)
    expand = (e_h == g * SSD_HEADS_PER_GROUP + (e_c >> 6)).astype(BF16)
    expand2 = jnp.concatenate([expand, expand], axis=0)
    expand3 = jnp.concatenate([expand2, expand], axis=0)
    to_lane0 = (LANES - g * SSD_HEADS_PER_GROUP) % LANES

    nt = (((1,), (1,)), ((), ()))
    tn = (((0,), (0,)), ((), ()))

    def chunk(c, carry):
        r0 = pl.multiple_of(c * ck, ck)
        window = xbc_ref[pl.ds(r0, 2 * ck), :]
        back = jnp.dot(shift, window, preferred_element_type=F32)
        acc = conv_b + conv_w[SSD_CONV - 1:SSD_CONV, :] * window[ck:2 * ck].astype(F32)
        for k in range(SSD_CONV - 1):
            acc = acc + conv_w[k:k + 1, :] * back[k * ck:(k + 1) * ck]
        conv = _silu(acc)
        xs = conv[:, 0:gw]
        bm = conv[:, gw:gw + SSD_STATE].astype(BF16)
        cm = conv[:, gw + SSD_STATE:cw].astype(BF16)

        scales = jnp.concatenate([sc_ref[i, pl.ds(r0, ck), :] for i in range(3)], axis=0)
        acum = acum_ref[pl.ds(r0, ck), :]
        ex = jnp.dot(jnp.concatenate(_split3(scales)[:2], axis=1), expand2,
                     preferred_element_type=F32)
        dt_x = ex[0:ck]
        wend_x = ex[ck:2 * ck]
        ea_x = ex[2 * ck:3 * ck]
        cd_x = jnp.dot(jnp.concatenate(_split3(cd_ref[c]), axis=1), expand3,
                       preferred_element_type=F32)[0:1]
        acum_g = pltpu.roll(acum, to_lane0, 1)
        acum_t = jnp.transpose(acum_g)

        scores = lax.dot_general(cm, bm, nt, preferred_element_type=F32)
        xdt = (xs * dt_x).astype(BF16)
        y_pairs = []
        for p in range(SSD_HEADS_PER_GROUP // 2):
            x_pair = xdt[:, p * LANES:(p + 1) * LANES]
            rhs = jnp.concatenate([jnp.where(low_half, x_pair, jnp.zeros_like(x_pair)),
                                   jnp.where(low_half, jnp.zeros_like(x_pair), x_pair)], axis=0)
            lhs = []
            for j in (2 * p, 2 * p + 1):
                a_col = jnp.broadcast_to(acum_g[:, j:j + 1], (ck, ck))
                seg = a_col - acum_t[j:j + 1, :]
                decay = jnp.exp(jnp.where(causal, seg, -jnp.inf))
                lhs.append((scores * decay).astype(BF16))
            y_pairs.append(jnp.dot(jnp.concatenate(lhs, axis=1), rhs, preferred_element_type=F32))
        y = jnp.concatenate(y_pairs, axis=1)

        state = state_ref[...]
        y = y + jnp.dot(cm, state.astype(BF16), preferred_element_type=F32) * ea_x
        state_ref[...] = state * cd_x + lax.dot_general(
            bm, (xs * wend_x).astype(BF16), tn, preferred_element_type=F32)

        y = y + xs * d_ref[...]
        y = y * _silu(z_ref[0, pl.ds(r0, ck), :].astype(F32))
        ms = jnp.mean(y * y, axis=-1, keepdims=True)
        y = y * lax.rsqrt(ms + EPS) * gn_ref[...]
        o_ref[pl.ds(r0, ck), :] = y.astype(o_ref.dtype)
        return carry

    lax.fori_loop(0, t_blk // ck, chunk, 0, unroll=4)
    xbc_ref[0:ck, :] = xbc_ref[t_blk:t_blk + ck, :]


def _ssd(zxbc, scales, acum, chunk_decay, conv_w, conv_b, d_x, gnorm, batch, seq, t_blk, tn_in):
    m = acum.shape[0]
    gw = SSD_GROUP_WIDTH
    d_inner = gnorm.shape[1]
    groups = d_inner // gw
    n_t = seq // t_blk
    per_tile = tn_in // gw
    x_tile0 = d_inner // tn_in
    b_tile = 2 * d_inner // tn_in
    c_tile = b_tile + 1
    assert groups * SSD_STATE == tn_in and tn_in % gw == 0
    xw0 = 0
    bw0 = d_inner // SSD_STATE
    cw0 = bw0 + groups

    def rows(b, t):
        return b * n_t + t

    return pl.pallas_call(
        functools.partial(_ssd_kernel, t_blk=t_blk),
        grid=(batch, groups, n_t),
        in_specs=[
            pl.BlockSpec((1, t_blk, gw), lambda b, g, t: (g // per_tile, rows(b, t), g % per_tile)),
            pl.BlockSpec((1, t_blk, gw), lambda b, g, t: (x_tile0 + g // per_tile, rows(b, t), g % per_tile)),
            pl.BlockSpec((1, t_blk, SSD_STATE), lambda b, g, t: (b_tile, rows(b, t), g)),
            pl.BlockSpec((1, t_blk, SSD_STATE), lambda b, g, t: (c_tile, rows(b, t), g)),
            pl.BlockSpec((3, t_blk, LANES), lambda b, g, t: (0, rows(b, t), 0)),
            pl.BlockSpec((t_blk, LANES), lambda b, g, t: (rows(b, t), 0)),
            pl.BlockSpec((t_blk // SSD_CHUNK, 8, LANES), lambda b, g, t: (rows(b, t), 0, 0)),
            pl.BlockSpec((SSD_CONV, gw), lambda b, g, t: (0, xw0 + g)),
            pl.BlockSpec((SSD_CONV, SSD_STATE), lambda b, g, t: (0, bw0 + g)),
            pl.BlockSpec((SSD_CONV, SSD_STATE), lambda b, g, t: (0, cw0 + g)),
            pl.BlockSpec((1, gw), lambda b, g, t: (0, xw0 + g)),
            pl.BlockSpec((1, SSD_STATE), lambda b, g, t: (0, bw0 + g)),
            pl.BlockSpec((1, SSD_STATE), lambda b, g, t: (0, cw0 + g)),
            pl.BlockSpec((1, gw), lambda b, g, t: (0, g)),
            pl.BlockSpec((1, gw), lambda b, g, t: (0, g)),
        ],
        out_specs=pl.BlockSpec((t_blk, gw), lambda b, g, t: (rows(b, t), g)),
        out_shape=jax.ShapeDtypeStruct((m, d_inner), BF16),
        scratch_shapes=[
            pltpu.VMEM((SSD_STATE, gw), F32),
            pltpu.VMEM((t_blk + SSD_CHUNK, gw + 2 * SSD_STATE), BF16),
        ],
        compiler_params=_cparams(("parallel", "parallel", "arbitrary")),
        name="ssd",
    )(zxbc, zxbc, zxbc, zxbc, scales, acum, chunk_decay, conv_w, conv_w, conv_w, conv_b, conv_b, conv_b,
      d_x, gnorm)


def _fcum_kernel(f_ref, b_ref, o_ref, ot_ref):
    x = f_ref[...] + b_ref[...]
    cum = _cumsum_rows(-_softplus(-x)) * LOG2E
    o_ref[0] = cum
    cum_t = jnp.transpose(cum)
    tq = ot_ref.shape[3]
    for i in range(ot_ref.shape[2]):
        ot_ref[0, :, i, :] = cum_t[:, i * tq:(i + 1) * tq]


def _fcum(f_logit, b_f, batch, seq, tq):
    return pl.pallas_call(
        _fcum_kernel,
        grid=(batch,),
        in_specs=[
            pl.BlockSpec((seq, LANES), lambda b: (b, 0)),
            pl.BlockSpec((1, LANES), lambda b: (0, 0)),
        ],
        out_specs=[
            pl.BlockSpec((1, seq, LANES), lambda b: (b, 0, 0)),
            pl.BlockSpec((1, LANES, seq // tq, tq), lambda b: (b, 0, 0, 0)),
        ],
        out_shape=[
            jax.ShapeDtypeStruct((batch, seq, LANES), F32),
            jax.ShapeDtypeStruct((batch, LANES, seq // tq, tq), F32),
        ],
        compiler_params=_cparams(("parallel",)),
        name="fcum",
    )(f_logit, b_f)


def _key_bias_columns(cum, head):
    r = lax.broadcasted_iota(jnp.int32, (LANES, LANES), 0)
    c = lax.broadcasted_iota(jnp.int32, (LANES, LANES), 1)
    out = None
    for i, piece in enumerate(_split3(cum)):
        sel = ((r == head) & (c == i)).astype(BF16)
        term = jnp.dot(piece, sel, preferred_element_type=F32)
        out = term if out is None else out + term
    return (-out).astype(BF16)


def _attn_kernel(q_ref, k_ref, v_ref, cq_ref, ck_ref, o_ref, kx_ref, vt_ref, s_ref, p_ref, acc_ref,
                 *, tq, tk, seq):
    head = pl.program_id(1)
    hd = FOX_HEAD_DIM
    nt = (((1,), (1,)), ((), ()))
    assert tq == 2 * tk

    for rc in range(seq // tk):
        r0 = rc * tk
        kx_ref[rc, :, 0:hd] = k_ref[0, r0:r0 + tk, :]
        kx_ref[rc, :, hd:2 * hd] = _key_bias_columns(ck_ref[0, r0:r0 + tk, :], head)
        vt_ref[rc] = jnp.transpose(v_ref[0, r0:r0 + tk, :].astype(F32)).astype(BF16)

    lane = lax.broadcasted_iota(jnp.int32, (tq, hd), 1)

    def queries(qb):
        rows = q_ref[0, pl.ds(pl.multiple_of(qb * tq, tq), tq), :]
        return jnp.concatenate([rows, (lane < 3).astype(BF16)], axis=1)

    def first_scores(qb):
        s_ref[0] = lax.dot_general(kx_ref[0], queries(qb), nt, preferred_element_type=F32)

    first_scores(0)

    def qblock(qi, carry_q):
        q0 = pl.multiple_of(qi * tq, tq)
        qx = queries(qi)
        cq = cq_ref[0, head % 8, pl.ds(qi, 1), :]

        def s1(t, slot):
            s_ref[slot] = lax.dot_general(kx_ref[t], qx, nt, preferred_element_type=F32)

        def s2(slot, m, l, key_offset=None):
            st = s_ref[slot]
            if key_offset is not None:
                k_idx = lax.broadcasted_iota(jnp.int32, st.shape, 0) + key_offset
                q_idx = lax.broadcasted_iota(jnp.int32, st.shape, 1)
                st = jnp.where(k_idx <= q_idx, st, -jnp.inf)
            m_new = jnp.maximum(m, jnp.max(st, axis=0, keepdims=True) + cq)
            alpha = jnp.exp2(m - m_new)
            p = jnp.exp2(st - (m_new - cq))
            p_ref[slot] = p.astype(BF16)
            return m_new, alpha * l + jnp.sum(p, axis=0, keepdims=True), alpha

        def s3(t, slot, alpha):
            acc_ref[...] = alpha * acc_ref[...] + jnp.dot(vt_ref[t], p_ref[slot],
                                                          preferred_element_type=F32)

        acc_ref[...] = jnp.zeros_like(acc_ref)
        p_ref[1] = jnp.zeros(p_ref.shape[1:], BF16)

        def body(j, carry):
            m, l, alpha = carry
            t = 2 * j
            s3(jnp.maximum(t - 1, 0), 1, alpha)
            s1(t + 1, 1)
            m, l, alpha = s2(0, m, l)
            s3(t, 0, alpha)
            s1(t + 2, 0)
            return s2(1, m, l)

        m0 = jnp.full((1, tq), -jnp.inf, F32)
        l0 = jnp.zeros((1, tq), F32)
        m, l, alpha = lax.fori_loop(0, qi, body, (m0, l0, jnp.ones((1, tq), F32)))
        t = 2 * qi
        s3(jnp.maximum(t - 1, 0), 1, alpha)
        st = lax.dot_general(kx_ref[t + 1], qx[tk:tq], nt, preferred_element_type=F32)
        m, l, alpha = s2(0, m, l, key_offset=0)
        s3(t, 0, alpha)
        k_idx = lax.broadcasted_iota(jnp.int32, st.shape, 0)
        q_idx = lax.broadcasted_iota(jnp.int32, st.shape, 1)
        st = jnp.where(k_idx <= q_idx, st, -jnp.inf)
        m_hi, l_hi, cq_hi = m[:, tk:tq], l[:, tk:tq], cq[:, tk:tq]
        m_new = jnp.maximum(m_hi, jnp.max(st, axis=0, keepdims=True) + cq_hi)
        alpha_hi = jnp.exp2(m_hi - m_new)
        p = jnp.exp2(st - (m_new - cq_hi))
        l_hi = alpha_hi * l_hi + jnp.sum(p, axis=0, keepdims=True)
        acc_ref[:, tk:tq] = alpha_hi * acc_ref[:, tk:tq] + jnp.dot(
            vt_ref[t + 1], p.astype(BF16), preferred_element_type=F32)
        l = jnp.concatenate([l[:, 0:tk], l_hi], axis=1)
        first_scores(jnp.minimum(qi + 1, seq // tq - 1))
        o_ref[pl.ds(q0, tq), :] = jnp.transpose(acc_ref[...] / l).astype(o_ref.dtype)
        return carry_q

    lax.fori_loop(0, seq // tq, qblock, 0)


def _attn(qkv, cum, cum_t, batch, seq, heads, tq, tn_in):
    m = qkv.shape[1]
    hd = FOX_HEAD_DIM
    per_tile = tn_in // hd
    tiles = heads // per_tile
    n_q = seq // tq
    tk = tq // 2
    return pl.pallas_call(
        functools.partial(_attn_kernel, tq=tq, tk=tk, seq=seq),
        grid=(batch, heads),
        in_specs=[
            pl.BlockSpec((1, seq, hd), lambda b, h: (h // per_tile, b, h % per_tile)),
            pl.BlockSpec((1, seq, hd), lambda b, h: (tiles + h // per_tile, b, h % per_tile)),
            pl.BlockSpec((1, seq, hd), lambda b, h: (2 * tiles + h // per_tile, b, h % per_tile)),
            pl.BlockSpec((1, 8, n_q, tq), lambda b, h: (b, h // 8, 0, 0)),
            pl.BlockSpec((1, seq, LANES), lambda b, h: (b, 0, 0)),
        ],
        out_specs=pl.BlockSpec((seq, hd), lambda b, h: (b, h)),
        out_shape=jax.ShapeDtypeStruct((m, heads * hd), BF16),
        scratch_shapes=[
            pltpu.VMEM((seq // tk, tk, 2 * hd), BF16),
            pltpu.VMEM((seq // tk, hd, tk), BF16),
            pltpu.VMEM((2, tk, tq), F32),
            pltpu.VMEM((2, tk, tq), BF16),
            pltpu.VMEM((hd, tq), F32),
        ],
        compiler_params=_cparams(("parallel", "parallel")),
        name="attn",
    )(qkv, qkv, qkv, cum_t, cum)


def _pad_cols(a, n):
    return jnp.pad(a, ((0, 0), (0, n - a.shape[1])))


def kernel(x, c, norm_mix, norm_mlp, w_ada, b_ada, w_up, w_down, ssd_w_in, ssd_conv_w, ssd_conv_b,
           ssd_dt_bias, ssd_A_log, ssd_D, ssd_gnorm, ssd_w_out, fox_w_in, fox_b_f, fox_w_out,
           final_norm):
    batch, seq, d = x.shape
    m = batch * seq
    depth = w_ada.shape[0]
    d_inner = ssd_gnorm.shape[1]
    ssd_heads = ssd_dt_bias.shape[1]
    fox_heads = fox_b_f.shape[1]
    fox_width = fox_heads * FOX_HEAD_DIM
    assert d_inner == ssd_heads * SSD_HEAD_DIM and ssd_heads <= LANES and fox_heads <= LANES

    tm = min(1024, seq)
    tn_in = (d_inner // SSD_GROUP_WIDTH) * SSD_STATE
    tn_out = min(512, d)
    tf = min(512, w_up.shape[2])
    t_blk = min(2048, seq)
    tq = min(512, seq)
    assert fox_width % tn_in == 0 and d_inner % tn_in == 0

    rows = 8
    c_pad = jnp.pad(c, ((0, rows - batch), (0, 0)))
    mod = _ada(c_pad, w_ada, b_ada, tn=min(1024, N_MOD * d))[:, :batch]
    mod = mod.reshape(depth, batch, N_MOD, d).transpose(0, 2, 1, 3).reshape(depth * N_MOD * batch, 1, d)

    def mod_row(layer, which):
        return (layer * N_MOD + which) * batch

    fox_w_in_t = jnp.transpose(fox_w_in, (0, 2, 1))
    fox_w_side = _cast_t_side(fox_w_in_t, 3 * fox_width)
    plain = {"w_up": w_up, "w_down": w_down, "ssd_w_out": ssd_w_out, "fox_w_out": fox_w_out}
    h = x.reshape(m, d)
    for layer in range(depth):
        j = layer // 2
        g_mix = norm_mix[layer].reshape(1, d)
        g_mlp = norm_mlp[layer].reshape(1, d)
        if layer % 2 == 0:
            n_main = 2 * d_inner + 2 * (d_inner // SSD_GROUP_WIDTH) * SSD_STATE
            ssd_w_in_t = jnp.transpose(ssd_w_in, (0, 2, 1))
            w_main, w_side = _cast_t_main(ssd_w_in_t, n_main, tn_in), _cast_t_side(ssd_w_in_t, n_main)
            ride = tuple(plain.values()) if layer == 0 else ()
            ride_t = ((fox_w_in_t, 3 * fox_width),) if layer == 0 else ()
            zxbc, dt_raw, *cast = _in_proj(h, g_mix, mod, mod_row(layer, 0), mod_row(layer, 1),
                                           w_main, j, n_main, w_side, seq, tm, tn_in,
                                           ride_along=ride, ride_along_t=ride_t)
            if cast:
                plain = dict(zip(plain, cast[:len(plain)]))
                fox_w_main = cast[-1]
            scales, acum, chunk_decay = _ssd_dt(dt_raw, _pad_cols(ssd_dt_bias[j].reshape(1, -1), LANES),
                                                _pad_cols(ssd_A_log[j].reshape(1, -1), LANES), t_blk)
            y = _ssd(zxbc, scales, acum, chunk_decay, ssd_conv_w[j], ssd_conv_b[j].reshape(1, -1),
                     jnp.repeat(ssd_D[j], SSD_HEAD_DIM).reshape(1, d_inner),
                     ssd_gnorm[j].reshape(1, d_inner), batch, seq, t_blk, tn_in)
            h = _out_proj(y, plain["ssd_w_out"], j, h, mod, mod_row(layer, 2), seq, tm, tn_out)
        else:
            qkv, f_logit = _in_proj(h, g_mix, mod, mod_row(layer, 0), mod_row(layer, 1),
                                    fox_w_main, j, 3 * fox_width, fox_w_side, seq, tm, tn_in,
                                    n_scaled=fox_width // tn_in, scale=FOX_HEAD_DIM ** -0.5 * LOG2E)
            cum, cum_t = _fcum(f_logit, _pad_cols(fox_b_f[j].reshape(1, -1), LANES), batch, seq, tq)
            o = _attn(qkv, cum, cum_t, batch, seq, fox_heads, tq, tn_in)
            h = _out_proj(o, plain["fox_w_out"], j, h, mod, mod_row(layer, 2), seq, tm, min(2 * tn_out, d))
        last = layer == depth - 1
        h = _mlp(h, g_mlp, mod, mod_row(layer, 3), mod_row(layer, 4), mod_row(layer, 5),
                 plain["w_up"], plain["w_down"], layer, final_norm.reshape(1, d),
                 seq, tm, tf, final_norm=last)
    return h.reshape(batch, seq, d)
```

```python
import functools

import jax
import jax.numpy as jnp
from jax import lax
from jax.experimental import pallas as pl
from jax.experimental.pallas import tpu as pltpu

F32 = jnp.float32
BF16 = jnp.bfloat16

EPS = 1e-6
N_MOD = 6
LANES = 128
SSD_HEAD_DIM = 64
SSD_STATE = 128
SSD_CONV = 4
SSD_HEADS_PER_GROUP = 8
SSD_GROUP_WIDTH = SSD_HEADS_PER_GROUP * SSD_HEAD_DIM
SSD_CHUNK = 128
FOX_HEAD_DIM = 128
LOG2E = 1.4426950408889634
VMEM_LIMIT = 56 * 1024 * 1024


def _cparams(sem):
    return pltpu.CompilerParams(dimension_semantics=sem, vmem_limit_bytes=VMEM_LIMIT)


def _silu(x):
    half = 0.5 * x
    return half + half * jnp.tanh(half)


def _softplus(x):
    return jnp.maximum(x, 0.0) + jnp.log1p(jnp.exp(-jnp.abs(x)))


def _split3(x):
    hi = x.astype(BF16)
    r1 = x - hi.astype(F32)
    mid = r1.astype(BF16)
    lo = (r1 - mid.astype(F32)).astype(BF16)
    return hi, mid, lo


def _cumsum_rows(x):
    n = x.shape[0]
    row = lax.broadcasted_iota(jnp.int32, x.shape, 0)
    k = 1
    while k < n:
        x = x + jnp.where(row >= k, pltpu.roll(x, k, 0), 0.0)
        k *= 2
    return x


def _norm_mod(x, g, shift, scale):
    ms = jnp.mean(x * x, axis=-1, keepdims=True)
    return x * lax.rsqrt(ms + EPS) * (g * (1.0 + scale)) + shift


def _cast_t_kernel(w_ref, o_ref):
    w = w_ref[0]
    pad = o_ref.shape[2] - w.shape[0]
    if pad:
        w = jnp.concatenate([w, jnp.zeros((pad, w.shape[1]), F32)], axis=0)
    o_ref[0] = jnp.transpose(w).astype(BF16)


def _cast_t_main(w_t, c_main, cols_blk):
    layers, c, k = w_t.shape
    assert c_main % cols_blk == 0
    return pl.pallas_call(
        _cast_t_kernel,
        grid=(layers, c_main // cols_blk),
        in_specs=[pl.BlockSpec((1, cols_blk, k), lambda l, i: (l, i, 0))],
        out_specs=pl.BlockSpec((1, k, cols_blk), lambda l, i: (l, 0, i)),
        out_shape=jax.ShapeDtypeStruct((layers, k, c_main), BF16),
        compiler_params=_cparams(("parallel", "parallel")),
        name="cast_t",
    )(w_t)


def _cast_t_side(w_t, c_main):
    layers, c, k = w_t.shape
    side = c - c_main
    assert 0 < side <= LANES and c_main % side == 0 and side % 8 == 0
    return pl.pallas_call(
        _cast_t_kernel,
        grid=(layers,),
        in_specs=[pl.BlockSpec((1, side, k), lambda l: (l, c_main // side, 0))],
        out_specs=pl.BlockSpec((1, k, LANES), lambda l: (l, 0, 0)),
        out_shape=jax.ShapeDtypeStruct((layers, k, LANES), BF16),
        compiler_params=_cparams(("parallel",)),
        name="cast_t_side",
    )(w_t)


def _ada_kernel(c_ref, w_ref, b_ref, o_ref):
    cond = _silu(c_ref[...])
    o_ref[0] = jnp.dot(cond.astype(BF16), w_ref[0].astype(BF16),
                       preferred_element_type=F32) + b_ref[0]


def _ada(c_pad, w_ada, b_ada, tn):
    depth, d, n = w_ada.shape
    rows = c_pad.shape[0]
    return pl.pallas_call(
        _ada_kernel,
        grid=(depth, n // tn),
        in_specs=[
            pl.BlockSpec((rows, d), lambda l, j: (0, 0)),
            pl.BlockSpec((1, d, tn), lambda l, j: (l, 0, j)),
            pl.BlockSpec((1, 1, tn), lambda l, j: (l, 0, j)),
        ],
        out_specs=pl.BlockSpec((1, rows, tn), lambda l, j: (l, 0, j)),
        out_shape=jax.ShapeDtypeStruct((depth, rows, n), F32),
        compiler_params=_cparams(("parallel", "parallel")),
        name="ada",
    )(c_pad, w_ada, b_ada.reshape(depth, 1, n))


def _in_proj_kernel(x_ref, g_ref, sh_ref, sc_ref, w_ref, wx_ref, *rest, n_scaled, scale, n_ride, n_ride_t):
    n_all = n_ride + n_ride_t
    ride_in = rest[:n_all]
    o_ref, ox_ref = rest[n_all:n_all + 2]
    ride_out = rest[n_all + 2:2 * n_all + 2]
    u_ref = rest[-1]
    j = pl.program_id(1)

    @pl.when(j == 0)
    def _():
        u = _norm_mod(x_ref[...], g_ref[...], sh_ref[0], sc_ref[0]).astype(BF16)
        u_ref[...] = u
        ox_ref[...] = jnp.dot(u, wx_ref[0], preferred_element_type=F32)

    acc = jnp.dot(u_ref[...], w_ref[0], preferred_element_type=F32)
    if n_scaled:
        acc = acc * jnp.where(j < n_scaled, scale, 1.0)
    o_ref[0] = acc.astype(o_ref.dtype)
    for src, dst in zip(ride_in[:n_ride], ride_out[:n_ride]):
        dst[...] = src[...].astype(BF16)
    for src, dst in zip(ride_in[n_ride:], ride_out[n_ride:]):
        dst[0] = jnp.transpose(src[0]).astype(BF16)


def _ride_rows(w, steps):
    layers, r, _ = w.shape
    for rb in range(16, r + 1, 16):
        if r % rb == 0 and layers * (r // rb) <= steps:
            return rb
    raise ValueError("weight too large to ride along")


def _in_proj(h, g, mod, sh_row, sc_row, w, w_layer, n, wx, seq, tm, tn, n_scaled=0, scale=1.0,
             ride_along=(), ride_along_t=()):
    m, d = h.shape
    per_b = seq // tm
    n_j = n // tn
    steps = (m // tm) * n_j
    ride_specs, ride_shapes = [], []
    for wr in ride_along:
        rb = _ride_rows(wr, steps)
        per_layer = wr.shape[1] // rb
        last = wr.shape[0] * per_layer - 1

        def slab(i, j, per_layer=per_layer, last=last):
            s = jnp.minimum(i * n_j + j, last)
            return (s // per_layer, s % per_layer, 0)

        ride_specs.append(pl.BlockSpec((1, rb, wr.shape[2]), slab))
        ride_shapes.append(jax.ShapeDtypeStruct(wr.shape, BF16))
    ride_in_specs, ride_out_specs = list(ride_specs), list(ride_specs)
    for wt, c_main in ride_along_t:
        per_layer = c_main // LANES
        last = wt.shape[0] * per_layer - 1
        assert last < steps

        def slab_t(i, j, per_layer=per_layer, last=last):
            s = jnp.minimum(i * n_j + j, last)
            return s // per_layer, s % per_layer

        ride_in_specs.append(pl.BlockSpec((1, LANES, wt.shape[2]), lambda i, j, f=slab_t: (f(i, j)[0], f(i, j)[1], 0)))
        ride_out_specs.append(pl.BlockSpec((1, wt.shape[2], LANES), lambda i, j, f=slab_t: (f(i, j)[0], 0, f(i, j)[1])))
        ride_shapes.append(jax.ShapeDtypeStruct((wt.shape[0], wt.shape[2], c_main), BF16))
    return pl.pallas_call(
        functools.partial(_in_proj_kernel, n_scaled=n_scaled, scale=scale, n_ride=len(ride_along),
                          n_ride_t=len(ride_along_t)),
        grid=(m // tm, n_j),
        in_specs=[
            pl.BlockSpec((tm, d), lambda i, j: (i, 0)),
            pl.BlockSpec((1, d), lambda i, j: (0, 0)),
            pl.BlockSpec((1, 1, d), lambda i, j: (sh_row + i // per_b, 0, 0)),
            pl.BlockSpec((1, 1, d), lambda i, j: (sc_row + i // per_b, 0, 0)),
            pl.BlockSpec((1, d, tn), lambda i, j: (w_layer, 0, j)),
            pl.BlockSpec((1, d, LANES), lambda i, j: (w_layer, 0, 0)),
        ] + ride_in_specs,
        out_specs=[
            pl.BlockSpec((1, tm, tn), lambda i, j: (j, i, 0)),
            pl.BlockSpec((tm, LANES), lambda i, j: (i, 0)),
        ] + ride_out_specs,
        out_shape=[
            jax.ShapeDtypeStruct((n_j, m, tn), BF16),
            jax.ShapeDtypeStruct((m, LANES), F32),
        ] + ride_shapes,
        scratch_shapes=[pltpu.VMEM((tm, d), BF16)],
        compiler_params=_cparams(("arbitrary", "arbitrary")),
        name="in_proj",
    )(h, g, mod, mod, w, wx, *ride_along, *[wt for wt, _ in ride_along_t])


def _out_proj_kernel(a_ref, w_ref, h_ref, gate_ref, o_ref):
    acc = jnp.dot(a_ref[...], w_ref[0], preferred_element_type=F32)
    o_ref[...] = h_ref[...] + gate_ref[0] * acc


def _out_proj(a, w, w_layer, h, mod, gate_row, seq, tm, tn):
    m, k = a.shape
    d = w.shape[2]
    per_b = seq // tm
    return pl.pallas_call(
        _out_proj_kernel,
        grid=(m // tm, d // tn),
        in_specs=[
            pl.BlockSpec((tm, k), lambda i, j: (i, 0)),
            pl.BlockSpec((1, k, tn), lambda i, j: (w_layer, 0, j)),
            pl.BlockSpec((tm, tn), lambda i, j: (i, j)),
            pl.BlockSpec((1, 1, tn), lambda i, j: (gate_row + i // per_b, 0, j)),
        ],
        out_specs=pl.BlockSpec((tm, tn), lambda i, j: (i, j)),
        out_shape=jax.ShapeDtypeStruct((m, d), F32),
        compiler_params=_cparams(("parallel", "arbitrary")),
        name="out_proj",
    )(a, w, h, mod)


def _mlp_kernel(h_ref, g_ref, sh_ref, sc_ref, gate_ref, wu_ref, wd_ref, fn_ref, o_ref, u_ref,
                *, final_norm):
    f = pl.program_id(1)

    @pl.when(f == 0)
    def _():
        u_ref[...] = _norm_mod(h_ref[...], g_ref[...], sh_ref[0], sc_ref[0]).astype(BF16)
        o_ref[...] = jnp.zeros_like(o_ref)

    a = jnp.dot(u_ref[...], wu_ref[0], preferred_element_type=F32)
    a = jnp.square(jnp.maximum(a, 0.0)).astype(BF16)
    o_ref[...] += jnp.dot(a, wd_ref[0], preferred_element_type=F32)

    @pl.when(f == pl.num_programs(1) - 1)
    def _():
        y = h_ref[...] + gate_ref[0] * o_ref[...]
        if final_norm:
            ms = jnp.mean(y * y, axis=-1, keepdims=True)
            y = y * lax.rsqrt(ms + EPS) * fn_ref[...]
        o_ref[...] = y


def _mlp(h, g, mod, sh_row, sc_row, gate_row, w_up, w_down, layer, fn, seq, tm, tf, final_norm):
    m, d = h.shape
    dff = w_up.shape[2]
    per_b = seq // tm
    return pl.pallas_call(
        functools.partial(_mlp_kernel, final_norm=final_norm),
        grid=(m // tm, dff // tf),
        in_specs=[
            pl.BlockSpec((tm, d), lambda i, f: (i, 0)),
            pl.BlockSpec((1, d), lambda i, f: (0, 0)),
            pl.BlockSpec((1, 1, d), lambda i, f: (sh_row + i // per_b, 0, 0)),
            pl.BlockSpec((1, 1, d), lambda i, f: (sc_row + i // per_b, 0, 0)),
            pl.BlockSpec((1, 1, d), lambda i, f: (gate_row + i // per_b, 0, 0)),
            pl.BlockSpec((1, d, tf), lambda i, f: (layer, 0, f)),
            pl.BlockSpec((1, tf, d), lambda i, f: (layer, f, 0)),
            pl.BlockSpec((1, d), lambda i, f: (0, 0)),
        ],
        out_specs=pl.BlockSpec((tm, d), lambda i, f: (i, 0)),
        out_shape=jax.ShapeDtypeStruct((m, d), F32),
        scratch_shapes=[pltpu.VMEM((tm, d), BF16)],
        compiler_params=_cparams(("parallel", "arbitrary")),
        name="mlp",
    )(h, g, mod, mod, mod, w_up, w_down, fn)


def _ssd_dt_kernel(dt_ref, dtb_ref, alog_ref, sc_ref, acum_ref, cd_ref):
    ck = SSD_CHUNK
    a_neg = -jnp.exp(alog_ref[...])
    for c in range(dt_ref.shape[0] // ck):
        rows = slice(c * ck, (c + 1) * ck)
        dt = _softplus(dt_ref[rows, :] + dtb_ref[...])
        acum = _cumsum_rows(dt * a_neg)
        a_last = acum[ck - 1:ck, :]
        sc_ref[0, rows, :] = dt
        sc_ref[1, rows, :] = dt * jnp.exp(a_last - acum)
        sc_ref[2, rows, :] = jnp.exp(acum)
        acum_ref[rows, :] = acum
        cd_ref[c] = jnp.broadcast_to(jnp.exp(a_last), (8, LANES))


def _ssd_dt(dt_raw, dtb, alog, blk):
    m = dt_raw.shape[0]
    per = blk // SSD_CHUNK
    return pl.pallas_call(
        _ssd_dt_kernel,
        grid=(m // blk,),
        in_specs=[
            pl.BlockSpec((blk, LANES), lambda i: (i, 0)),
            pl.BlockSpec((1, LANES), lambda i: (0, 0)),
            pl.BlockSpec((1, LANES), lambda i: (0, 0)),
        ],
        out_specs=[
            pl.BlockSpec((3, blk, LANES), lambda i: (0, i, 0)),
            pl.BlockSpec((blk, LANES), lambda i: (i, 0)),
            pl.BlockSpec((per, 8, LANES), lambda i: (i, 0, 0)),
        ],
        out_shape=[
            jax.ShapeDtypeStruct((3, m, LANES), F32),
            jax.ShapeDtypeStruct((m, LANES), F32),
            jax.ShapeDtypeStruct((m // SSD_CHUNK, 8, LANES), F32),
        ],
        compiler_params=_cparams(("parallel",)),
        name="ssd_dt",
    )(dt_raw, dtb, alog)


def _ssd_kernel(z_ref, x_ref, b_ref, c_ref, sc_ref, acum_ref, cd_ref, wx_ref, wb_ref, wc_ref,
                bx_ref, bb_ref, bc_ref, d_ref, gn_ref, o_ref, state_ref, xbc_ref, *, t_blk):
    g = pl.program_id(1)
    t = pl.program_id(2)
    ck = SSD_CHUNK
    gw = SSD_GROUP_WIDTH
    cw = gw + 2 * SSD_STATE

    @pl.when(t == 0)
    def _():
        state_ref[...] = jnp.zeros_like(state_ref)
        xbc_ref[0:ck, :] = jnp.zeros((ck, cw), BF16)

    xbc_ref[ck:ck + t_blk, 0:gw] = x_ref[0]
    xbc_ref[ck:ck + t_blk, gw:gw + SSD_STATE] = b_ref[0]
    xbc_ref[ck:ck + t_blk, gw + SSD_STATE:cw] = c_ref[0]
    conv_w = jnp.concatenate([wx_ref[...], wb_ref[...], wc_ref[...]], axis=1)
    conv_b = jnp.concatenate([bx_ref[...], bb_ref[...], bc_ref[...]], axis=1)

    sh_r = lax.broadcasted_iota(jnp.int32, ((SSD_CONV - 1) * ck, 2 * ck), 0)
    sh_c = lax.broadcasted_iota(jnp.int32, ((SSD_CONV - 1) * ck, 2 * ck), 1)
    shift = (sh_c == (sh_r & (ck - 1)) + (ck - (SSD_CONV - 1)) + (sh_r >> 7)).astype(BF16)
    assert ck == 128

    row = lax.broadcasted_iota(jnp.int32, (ck, ck), 0)
    col = lax.broadcasted_iota(jnp.int32, (ck, ck), 1)
    causal = row >= col
    low_half = col < SSD_HEAD_DIM
    e_h = lax.broadcasted_iota(jnp.int32, (LANES, gw), 0)
    e_c = lax.broadcasted_iota(jnp.int32, (LANES, gw), 1)
    expand = (e_h == g * SSD_HEADS_PER_GROUP + (e_c >> 6)).astype(BF16)
    expand2 = jnp.concatenate([expand, expand], axis=0)
    expand3 = jnp.concatenate([expand2, expand], axis=0)
    to_lane0 = (LANES - g * SSD_HEADS_PER_GROUP) % LANES

    nt = (((1,), (1,)), ((), ()))
    tn = (((0,), (0,)), ((), ()))

    def chunk(c, carry):
        r0 = pl.multiple_of(c * ck, ck)
        window = xbc_ref[pl.ds(r0, 2 * ck), :]
        back = jnp.dot(shift, window, preferred_element_type=F32)
        acc = conv_b + conv_w[SSD_CONV - 1:SSD_CONV, :] * window[ck:2 * ck].astype(F32)
        for k in range(SSD_CONV - 1):
            acc = acc + conv_w[k:k + 1, :] * back[k * ck:(k + 1) * ck]
        conv = _silu(acc)
        xs = conv[:, 0:gw]
        bm = conv[:, gw:gw + SSD_STATE].astype(BF16)
        cm = conv[:, gw + SSD_STATE:cw].astype(BF16)

        scales = jnp.concatenate([sc_ref[i, pl.ds(r0, ck), :] for i in range(3)], axis=0)
        acum = acum_ref[pl.ds(r0, ck), :]
        ex = jnp.dot(jnp.concatenate(_split3(scales)[:2], axis=1), expand2,
                     preferred_element_type=F32)
        dt_x = ex[0:ck]
        wend_x = ex[ck:2 * ck]
        ea_x = ex[2 * ck:3 * ck]
        cd_x = jnp.dot(jnp.concatenate(_split3(cd_ref[c]), axis=1), expand3,
                       preferred_element_type=F32)[0:1]
        acum_g = pltpu.roll(acum, to_lane0, 1)
        acum_t = jnp.transpose(acum_g)

        scores = lax.dot_general(cm, bm, nt, preferred_element_type=F32)
        xdt = (xs * dt_x).astype(BF16)
        y_pairs = []
        for p in range(SSD_HEADS_PER_GROUP // 2):
            x_pair = xdt[:, p * LANES:(p + 1) * LANES]
            rhs = jnp.concatenate([jnp.where(low_half, x_pair, jnp.zeros_like(x_pair)),
                                   jnp.where(low_half, jnp.zeros_like(x_pair), x_pair)], axis=0)
            lhs = []
            for j in (2 * p, 2 * p + 1):
                a_col = jnp.broadcast_to(acum_g[:, j:j + 1], (ck, ck))
                seg = a_col - acum_t[j:j + 1, :]
                decay = jnp.exp(jnp.where(causal, seg, -jnp.inf))
                lhs.append((scores * decay).astype(BF16))
            y_pairs.append(jnp.dot(jnp.concatenate(lhs, axis=1), rhs, preferred_element_type=F32))
        y = jnp.concatenate(y_pairs, axis=1)

        state = state_ref[...]
        y = y + jnp.dot(cm, state.astype(BF16), preferred_element_type=F32) * ea_x
        state_ref[...] = state * cd_x + lax.dot_general(
            bm, (xs * wend_x).astype(BF16), tn, preferred_element_type=F32)

        y = y + xs * d_ref[...]
        y = y * _silu(z_ref[0, pl.ds(r0, ck), :].astype(F32))
        ms = jnp.mean(y * y, axis=-1, keepdims=True)
        y = y * lax.rsqrt(ms + EPS) * gn_ref[...]
        o_ref[pl.ds(r0, ck), :] = y.astype(o_ref.dtype)
        return carry

    lax.fori_loop(0, t_blk // ck, chunk, 0, unroll=8)
    xbc_ref[0:ck, :] = xbc_ref[t_blk:t_blk + ck, :]


def _ssd(zxbc, scales, acum, chunk_decay, conv_w, conv_b, d_x, gnorm, batch, seq, t_blk, tn_in):
    m = acum.shape[0]
    gw = SSD_GROUP_WIDTH
    d_inner = gnorm.shape[1]
    groups = d_inner // gw
    n_t = seq // t_blk
    per_tile = tn_in // gw
    x_tile0 = d_inner // tn_in
    b_tile = 2 * d_inner // tn_in
    c_tile = b_tile + 1
    assert groups * SSD_STATE == tn_in and tn_in % gw == 0
    xw0 = 0
    bw0 = d_inner // SSD_STATE
    cw0 = bw0 + groups

    def rows(b, t):
        return b * n_t + t

    return pl.pallas_call(
        functools.partial(_ssd_kernel, t_blk=t_blk),
        grid=(batch, groups, n_t),
        in_specs=[
            pl.BlockSpec((1, t_blk, gw), lambda b, g, t: (g // per_tile, rows(b, t), g % per_tile)),
            pl.BlockSpec((1, t_blk, gw), lambda b, g, t: (x_tile0 + g // per_tile, rows(b, t), g % per_tile)),
            pl.BlockSpec((1, t_blk, SSD_STATE), lambda b, g, t: (b_tile, rows(b, t), g)),
            pl.BlockSpec((1, t_blk, SSD_STATE), lambda b, g, t: (c_tile, rows(b, t), g)),
            pl.BlockSpec((3, t_blk, LANES), lambda b, g, t: (0, rows(b, t), 0)),
            pl.BlockSpec((t_blk, LANES), lambda b, g, t: (rows(b, t), 0)),
            pl.BlockSpec((t_blk // SSD_CHUNK, 8, LANES), lambda b, g, t: (rows(b, t), 0, 0)),
            pl.BlockSpec((SSD_CONV, gw), lambda b, g, t: (0, xw0 + g)),
            pl.BlockSpec((SSD_CONV, SSD_STATE), lambda b, g, t: (0, bw0 + g)),
            pl.BlockSpec((SSD_CONV, SSD_STATE), lambda b, g, t: (0, cw0 + g)),
            pl.BlockSpec((1, gw), lambda b, g, t: (0, xw0 + g)),
            pl.BlockSpec((1, SSD_STATE), lambda b, g, t: (0, bw0 + g)),
            pl.BlockSpec((1, SSD_STATE), lambda b, g, t: (0, cw0 + g)),
            pl.BlockSpec((1, gw), lambda b, g, t: (0, g)),
            pl.BlockSpec((1, gw), lambda b, g, t: (0, g)),
        ],
        out_specs=pl.BlockSpec((t_blk, gw), lambda b, g, t: (rows(b, t), g)),
        out_shape=jax.ShapeDtypeStruct((m, d_inner), BF16),
        scratch_shapes=[
            pltpu.VMEM((SSD_STATE, gw), F32),
            pltpu.VMEM((t_blk + SSD_CHUNK, gw + 2 * SSD_STATE), BF16),
        ],
        compiler_params=_cparams(("parallel", "parallel", "arbitrary")),
        name="ssd",
    )(zxbc, zxbc, zxbc, zxbc, scales, acum, chunk_decay, conv_w, conv_w, conv_w, conv_b, conv_b, conv_b,
      d_x, gnorm)


def _fcum_kernel(f_ref, b_ref, o_ref, ot_ref):
    x = f_ref[...] + b_ref[...]
    cum = _cumsum_rows(-_softplus(-x)) * LOG2E
    o_ref[0] = cum
    cum_t = jnp.transpose(cum)
    tq = ot_ref.shape[3]
    for i in range(ot_ref.shape[2]):
        ot_ref[0, :, i, :] = cum_t[:, i * tq:(i + 1) * tq]


def _fcum(f_logit, b_f, batch, seq, tq):
    return pl.pallas_call(
        _fcum_kernel,
        grid=(batch,),
        in_specs=[
            pl.BlockSpec((seq, LANES), lambda b: (b, 0)),
            pl.BlockSpec((1, LANES), lambda b: (0, 0)),
        ],
        out_specs=[
            pl.BlockSpec((1, seq, LANES), lambda b: (b, 0, 0)),
            pl.BlockSpec((1, LANES, seq // tq, tq), lambda b: (b, 0, 0, 0)),
        ],
        out_shape=[
            jax.ShapeDtypeStruct((batch, seq, LANES), F32),
            jax.ShapeDtypeStruct((batch, LANES, seq // tq, tq), F32),
        ],
        compiler_params=_cparams(("parallel",)),
        name="fcum",
    )(f_logit, b_f)


def _key_bias_columns(cum, head):
    r = lax.broadcasted_iota(jnp.int32, (LANES, LANES), 0)
    c = lax.broadcasted_iota(jnp.int32, (LANES, LANES), 1)
    out = None
    for i, piece in enumerate(_split3(cum)):
        sel = ((r == head) & (c == i)).astype(BF16)
        term = jnp.dot(piece, sel, preferred_element_type=F32)
        out = term if out is None else out + term
    return (-out).astype(BF16)


def _attn_kernel(q_ref, k_ref, v_ref, cq_ref, ck_ref, o_ref, kx_ref, vt_ref, s_ref, p_ref, acc_ref,
                 *, tq, tk, seq):
    head = pl.program_id(1)
    hd = FOX_HEAD_DIM
    nt = (((1,), (1,)), ((), ()))
    assert tq == 2 * tk

    for rc in range(seq // tk):
        r0 = rc * tk
        kx_ref[rc, :, 0:hd] = k_ref[0, r0:r0 + tk, :]
        kx_ref[rc, :, hd:2 * hd] = _key_bias_columns(ck_ref[0, r0:r0 + tk, :], head)
        vt_ref[rc] = jnp.transpose(v_ref[0, r0:r0 + tk, :].astype(F32)).astype(BF16)

    lane = lax.broadcasted_iota(jnp.int32, (tq, hd), 1)

    def queries(qb):
        rows = q_ref[0, pl.ds(pl.multiple_of(qb * tq, tq), tq), :]
        return jnp.concatenate([rows, (lane < 3).astype(BF16)], axis=1)

    def first_scores(qb):
        s_ref[0] = lax.dot_general(kx_ref[0], queries(qb), nt, preferred_element_type=F32)

    first_scores(0)

    def qblock(qi, carry_q):
        q0 = pl.multiple_of(qi * tq, tq)
        qx = queries(qi)
        cq = cq_ref[0, head % 8, pl.ds(qi, 1), :]

        def s1(t, slot):
            s_ref[slot] = lax.dot_general(kx_ref[t], qx, nt, preferred_element_type=F32)

        def s2(slot, m, l, key_offset=None):
            st = s_ref[slot]
            if key_offset is not None:
                k_idx = lax.broadcasted_iota(jnp.int32, st.shape, 0) + key_offset
                q_idx = lax.broadcasted_iota(jnp.int32, st.shape, 1)
                st = jnp.where(k_idx <= q_idx, st, -jnp.inf)
            m_new = jnp.maximum(m, jnp.max(st, axis=0, keepdims=True) + cq)
            alpha = jnp.exp2(m - m_new)
            p = jnp.exp2(st - (m_new - cq))
            p_ref[slot] = p.astype(BF16)
            return m_new, alpha * l + jnp.sum(p, axis=0, keepdims=True), alpha

        def s3(t, slot, alpha):
            acc_ref[...] = alpha * acc_ref[...] + jnp.dot(vt_ref[t], p_ref[slot],
                                                          preferred_element_type=F32)

        acc_ref[...] = jnp.zeros_like(acc_ref)
        p_ref[1] = jnp.zeros(p_ref.shape[1:], BF16)

        def body(j, carry):
            m, l, alpha = carry
            t = 2 * j
            s3(jnp.maximum(t - 1, 0), 1, alpha)
            s1(t + 1, 1)
            m, l, alpha = s2(0, m, l)
            s3(t, 0, alpha)
            s1(t + 2, 0)
            return s2(1, m, l)

        m0 = jnp.full((1, tq), -jnp.inf, F32)
        l0 = jnp.zeros((1, tq), F32)
        m, l, alpha = lax.fori_loop(0, qi, body, (m0, l0, jnp.ones((1, tq), F32)))
        t = 2 * qi
        s3(jnp.maximum(t - 1, 0), 1, alpha)
        st = lax.dot_general(kx_ref[t + 1], qx[tk:tq], nt, preferred_element_type=F32)
        m, l, alpha = s2(0, m, l, key_offset=0)
        s3(t, 0, alpha)
        k_idx = lax.broadcasted_iota(jnp.int32, st.shape, 0)
        q_idx = lax.broadcasted_iota(jnp.int32, st.shape, 1)
        st = jnp.where(k_idx <= q_idx, st, -jnp.inf)
        m_hi, l_hi, cq_hi = m[:, tk:tq], l[:, tk:tq], cq[:, tk:tq]
        m_new = jnp.maximum(m_hi, jnp.max(st, axis=0, keepdims=True) + cq_hi)
        alpha_hi = jnp.exp2(m_hi - m_new)
        p = jnp.exp2(st - (m_new - cq_hi))
        l_hi = alpha_hi * l_hi + jnp.sum(p, axis=0, keepdims=True)
        acc_ref[:, tk:tq] = alpha_hi * acc_ref[:, tk:tq] + jnp.dot(
            vt_ref[t + 1], p.astype(BF16), preferred_element_type=F32)
        l = jnp.concatenate([l[:, 0:tk], l_hi], axis=1)
        first_scores(jnp.minimum(qi + 1, seq // tq - 1))
        o_ref[pl.ds(q0, tq), :] = jnp.transpose(acc_ref[...] / l).astype(o_ref.dtype)
        return carry_q

    lax.fori_loop(0, seq // tq, qblock, 0)


def _attn(qkv, cum, cum_t, batch, seq, heads, tq, tn_in):
    m = qkv.shape[1]
    hd = FOX_HEAD_DIM
    per_tile = tn_in // hd
    tiles = heads // per_tile
    n_q = seq // tq
    tk = tq // 2
    return pl.pallas_call(
        functools.partial(_attn_kernel, tq=tq, tk=tk, seq=seq),
        grid=(batch, heads),
        in_specs=[
            pl.BlockSpec((1, seq, hd), lambda b, h: (h // per_tile, b, h % per_tile)),
            pl.BlockSpec((1, seq, hd), lambda b, h: (tiles + h // per_tile, b, h % per_tile)),
            pl.BlockSpec((1, seq, hd), lambda b, h: (2 * tiles + h // per_tile, b, h % per_tile)),
            pl.BlockSpec((1, 8, n_q, tq), lambda b, h: (b, h // 8, 0, 0)),
            pl.BlockSpec((1, seq, LANES), lambda b, h: (b, 0, 0)),
        ],
        out_specs=pl.BlockSpec((seq, hd), lambda b, h: (b, h)),
        out_shape=jax.ShapeDtypeStruct((m, heads * hd), BF16),
        scratch_shapes=[
            pltpu.VMEM((seq // tk, tk, 2 * hd), BF16),
            pltpu.VMEM((seq // tk, hd, tk), BF16),
            pltpu.VMEM((2, tk, tq), F32),
            pltpu.VMEM((2, tk, tq), BF16),
            pltpu.VMEM((hd, tq), F32),
        ],
        compiler_params=_cparams(("parallel", "parallel")),
        name="attn",
    )(qkv, qkv, qkv, cum_t, cum)


def _pad_cols(a, n):
    return jnp.pad(a, ((0, 0), (0, n - a.shape[1])))


def kernel(x, c, norm_mix, norm_mlp, w_ada, b_ada, w_up, w_down, ssd_w_in, ssd_conv_w, ssd_conv_b,
           ssd_dt_bias, ssd_A_log, ssd_D, ssd_gnorm, ssd_w_out, fox_w_in, fox_b_f, fox_w_out,
           final_norm):
    batch, seq, d = x.shape
    m = batch * seq
    depth = w_ada.shape[0]
    d_inner = ssd_gnorm.shape[1]
    ssd_heads = ssd_dt_bias.shape[1]
    fox_heads = fox_b_f.shape[1]
    fox_width = fox_heads * FOX_HEAD_DIM
    assert d_inner == ssd_heads * SSD_HEAD_DIM and ssd_heads <= LANES and fox_heads <= LANES

    tm = min(1024, seq)
    tn_in = (d_inner // SSD_GROUP_WIDTH) * SSD_STATE
    tn_out = min(512, d)
    tf = min(512, w_up.shape[2])
    t_blk = min(2048, seq)
    tq = min(512, seq)
    assert fox_width % tn_in == 0 and d_inner % tn_in == 0

    rows = 8
    c_pad = jnp.pad(c, ((0, rows - batch), (0, 0)))
    mod = _ada(c_pad, w_ada, b_ada, tn=min(1024, N_MOD * d))[:, :batch]
    mod = mod.reshape(depth, batch, N_MOD, d).transpose(0, 2, 1, 3).reshape(depth * N_MOD * batch, 1, d)

    def mod_row(layer, which):
        return (layer * N_MOD + which) * batch

    fox_w_in_t = jnp.transpose(fox_w_in, (0, 2, 1))
    fox_w_side = _cast_t_side(fox_w_in_t, 3 * fox_width)
    plain = {"w_up": w_up, "w_down": w_down, "ssd_w_out": ssd_w_out, "fox_w_out": fox_w_out}
    h = x.reshape(m, d)
    for layer in range(depth):
        j = layer // 2
        g_mix = norm_mix[layer].reshape(1, d)
        g_mlp = norm_mlp[layer].reshape(1, d)
        if layer % 2 == 0:
            n_main = 2 * d_inner + 2 * (d_inner // SSD_GROUP_WIDTH) * SSD_STATE
            ssd_w_in_t = jnp.transpose(ssd_w_in, (0, 2, 1))
            w_main, w_side = _cast_t_main(ssd_w_in_t, n_main, tn_in), _cast_t_side(ssd_w_in_t, n_main)
            ride = tuple(plain.values()) if layer == 0 else ()
            ride_t = ((fox_w_in_t, 3 * fox_width),) if layer == 0 else ()
            zxbc, dt_raw, *cast = _in_proj(h, g_mix, mod, mod_row(layer, 0), mod_row(layer, 1),
                                           w_main, j, n_main, w_side, seq, tm, tn_in,
                                           ride_along=ride, ride_along_t=ride_t)
            if cast:
                plain = dict(zip(plain, cast[:len(plain)]))
                fox_w_main = cast[-1]
            scales, acum, chunk_decay = _ssd_dt(dt_raw, _pad_cols(ssd_dt_bias[j].reshape(1, -1), LANES),
                                                _pad_cols(ssd_A_log[j].reshape(1, -1), LANES), t_blk)
            y = _ssd(zxbc, scales, acum, chunk_decay, ssd_conv_w[j], ssd_conv_b[j].reshape(1, -1),
                     jnp.repeat(ssd_D[j], SSD_HEAD_DIM).reshape(1, d_inner),
                     ssd_gnorm[j].reshape(1, d_inner), batch, seq, t_blk, tn_in)
            h = _out_proj(y, plain["ssd_w_out"], j, h, mod, mod_row(layer, 2), seq, tm, tn_out)
        else:
            qkv, f_logit = _in_proj(h, g_mix, mod, mod_row(layer, 0), mod_row(layer, 1),
                                    fox_w_main, j, 3 * fox_width, fox_w_side, seq, tm, tn_in,
                                    n_scaled=fox_width // tn_in, scale=FOX_HEAD_DIM ** -0.5 * LOG2E)
            cum, cum_t = _fcum(f_logit, _pad_cols(fox_b_f[j].reshape(1, -1), LANES), batch, seq, tq)
            o = _attn(qkv, cum, cum_t, batch, seq, fox_heads, tq, tn_in)
            h = _out_proj(o, plain["fox_w_out"], j, h, mod, mod_row(layer, 2), seq, tm, min(2 * tn_out, d))
        last = layer == depth - 1
        h = _mlp(h, g_mlp, mod, mod_row(layer, 3), mod_row(layer, 4), mod_row(layer, 5),
                 plain["w_up"], plain["w_down"], layer, final_norm.reshape(1, d),
                 seq, tm, tf, final_norm=last)
    return h.reshape(batch, seq, d)
```
